```python
import math
import jax, jax.numpy as jnp
from jax import lax
import numpy as np

D_MODEL = 2048
BATCH = 2
SEQ = 8192
DEPTH = 4

CHUNK = 64
D_MIX = D_MODEL
DA_QK_DIM = 64
DA_V_DIM = 2 * DA_QK_DIM
DA_WIDTH = D_MIX // 2
DA_HEADS = DA_WIDTH // DA_V_DIM
DA_QK_COLS = DA_HEADS * 2 * DA_QK_DIM
CB_WIDTH = D_MIX // 4
CB_HEAD_DIM = 128
CB_HEADS = CB_WIDTH // CB_HEAD_DIM
CB_LEFT_CHUNKS = 8
CB_BAND = CB_LEFT_CHUNKS + 1
REL_CLIP = 128
CV_WIDTH = D_MIX - DA_WIDTH - CB_WIDTH
CONV_WIDTH = 31
COL_SIZES = (DA_QK_COLS, DA_QK_COLS, DA_WIDTH, CB_WIDTH, CB_WIDTH, CB_WIDTH, CV_WIDTH, CV_WIDTH)
IN_COLS = sum(COL_SIZES)
ROPE_THETA = 500000.0
ROPE_DIM = DA_QK_DIM // 4
Q_BLOCK = 128
D_FF = 5632
N_EXPERTS = 8
TOP_K = 2
D_FF_EXPERT = 5632
N_DENSE = (DEPTH + 1) // 2
N_MOE = DEPTH // 2
NORM_EPS = 1e-6
NEG = -1e30

kernel_name = "hybrid_diffattn_chunkband_conformerconv_moe"


def rms_norm(x, w):
    xf = x.astype(jnp.float32)
    y = xf * lax.rsqrt(jnp.mean(xf * xf, axis=-1, keepdims=True) + NORM_EPS)
    return (y * w.astype(jnp.float32)).astype(x.dtype)


def layer_norm(x, w, b):
    xf = x.astype(jnp.float32)
    mu = jnp.mean(xf, axis=-1, keepdims=True)
    xc = xf - mu
    y = xc * lax.rsqrt(jnp.mean(xc * xc, axis=-1, keepdims=True) + NORM_EPS)
    return (y * w.astype(jnp.float32) + b.astype(jnp.float32)).astype(x.dtype)


def rope_partial(x, cos, sin):
    half = ROPE_DIM // 2
    x1 = x[..., :half]
    x2 = x[..., half:ROPE_DIM]
    return jnp.concatenate([x1 * cos - x2 * sin, x2 * cos + x1 * sin, x[..., ROPE_DIM:]], axis=-1)


def diff_attention(q, k, v, lam, lam_init, subln_w):
    B, S, H, _, d = q.shape
    nqb = S // Q_BLOCK
    scale = d ** -0.5
    kt = k.transpose(0, 2, 3, 1, 4)
    vt = v.transpose(0, 2, 1, 3)
    qb = q.reshape(B, nqb, Q_BLOCK, H, 2, d).transpose(1, 0, 3, 4, 2, 5)
    key_chunk = jnp.arange(S) // CHUNK

    def block(args):
        qblk, bi = args
        s = jnp.einsum('bhmqd,bhmkd->bhmqk', qblk, kt).astype(jnp.float32) * scale
        q_chunk = (bi * Q_BLOCK + jnp.arange(Q_BLOCK)) // CHUNK
        mask = key_chunk[None, :] <= q_chunk[:, None]
        p = jax.nn.softmax(jnp.where(mask, s, NEG), axis=-1)
        a = p[:, :, 0] - lam * p[:, :, 1]
        return jnp.einsum('bhqk,bhkd->bhqd', a.astype(vt.dtype), vt)

    o = lax.map(block, (qb, jnp.arange(nqb)))
    o = o.transpose(1, 0, 3, 2, 4).reshape(B, S, H, v.shape[-1])
    o = rms_norm(o, subln_w) * (1.0 - lam_init)
    return o.reshape(B, S, H * v.shape[-1])


def chunk_band_attention(q, k, v, rel_bias):
    B, S, H, d = q.shape
    nc = S // CHUNK
    to_chunks = lambda t: t.reshape(B, nc, CHUNK, H, d).transpose(0, 3, 1, 2, 4)
    qc, kc, vc = to_chunks(q), to_chunks(k), to_chunks(v)
    pad = ((0, 0), (0, 0), (CB_LEFT_CHUNKS, 0), (0, 0), (0, 0))
    kp, vp = jnp.pad(kc, pad), jnp.pad(vc, pad)
    kband = jnp.concatenate([kp[:, :, s:s + nc] for s in range(CB_BAND)], axis=3)
    vband = jnp.concatenate([vp[:, :, s:s + nc] for s in range(CB_BAND)], axis=3)
    qi = jnp.arange(CHUNK)[:, None]
    kj = jnp.arange(CB_BAND * CHUNK)[None, :]
    dist = qi + CB_LEFT_CHUNKS * CHUNK - kj
    idx = jnp.clip(dist, -REL_CLIP, REL_CLIP) + REL_CLIP
    bias = rel_bias[:, idx].astype(jnp.float32)
    s = jnp.einsum('bhnqd,bhnkd->bhnqk', qc, kband).astype(jnp.float32) * (d ** -0.5) + bias[None, :, None]
    valid = (jnp.arange(nc)[:, None] + jnp.arange(CB_BAND)[None, :]) >= CB_LEFT_CHUNKS
    valid = jnp.repeat(valid, CHUNK, axis=1)
    p = jax.nn.softmax(jnp.where(valid[None, None, :, None, :], s, NEG), axis=-1)
    o = jnp.einsum('bhnqk,bhnkd->bhnqd', p.astype(vband.dtype), vband)
    return o.transpose(0, 2, 3, 1, 4).reshape(B, S, H * d)


def conv_module(a, g, dw_w, dw_b, ln_w, ln_b):
    u = a * jax.nn.sigmoid(g)
    y = lax.conv_general_dilated(u, dw_w[:, None, :].astype(u.dtype), window_strides=(1,),
                                 padding=[(CONV_WIDTH - 1, 0)], dimension_numbers=('NWC', 'WIO', 'NWC'),
                                 feature_group_count=CV_WIDTH) + dw_b.astype(u.dtype)
    return jax.nn.silu(layer_norm(y, ln_w, ln_b))


def swiglu(x, w1, w3, w2):
    return (jax.nn.silu(x @ w1) * (x @ w3)) @ w2


def moe_swiglu(x, router_w, router_b, w1, w3, w2):
    shp = x.shape
    xf = x.reshape(-1, shp[-1])
    logits = (xf @ router_w).astype(jnp.float32) + router_b.astype(jnp.float32)
    vals, idx = lax.top_k(logits, TOP_K)
    gates = jax.nn.softmax(vals, axis=-1)
    combine = jnp.sum(jax.nn.one_hot(idx, N_EXPERTS, dtype=jnp.float32) * gates[..., None], axis=1)
    y = jnp.zeros_like(xf)
    for e in range(N_EXPERTS):
        y = y + combine[:, e:e + 1].astype(xf.dtype) * swiglu(xf, w1[e], w3[e], w2[e])
    return y.reshape(shp)


def setup_inputs(seed: int = 0) -> dict:
    key = jax.random.key(seed)
    ks = jax.random.split(key, 24)
    nrm = lambda k, shape, s: jax.random.normal(k, shape, jnp.float32) * s
    offsets = jax.random.randint(ks[1], (BATCH, 1), 0, 64) * CHUNK
    positions = (offsets + jnp.arange(SEQ)[None, :]).astype(jnp.int32)
    return {
        "x": nrm(ks[0], (BATCH, SEQ, D_MODEL), 1.0),
        "positions": positions,
        "attn_norm_w": 1.0 + nrm(ks[2], (DEPTH, D_MODEL), 0.02),
        "w_in": nrm(ks[3], (DEPTH, D_MODEL, IN_COLS), D_MODEL ** -0.5),
        "diff_lambda": nrm(ks[4], (DEPTH, 4, DA_QK_DIM), 0.1),
        "diff_subln_w": 1.0 + nrm(ks[5], (DEPTH, DA_V_DIM), 0.02),
        "rel_bias": nrm(ks[6], (DEPTH, CB_HEADS, 2 * REL_CLIP + 1), 0.2),
        "conv_dw_w": nrm(ks[7], (DEPTH, CONV_WIDTH, CV_WIDTH), CONV_WIDTH ** -0.5),
        "conv_dw_b": nrm(ks[8], (DEPTH, CV_WIDTH), 0.02),
        "conv_ln_w": 1.0 + nrm(ks[9], (DEPTH, CV_WIDTH), 0.02),
        "conv_ln_b": nrm(ks[10], (DEPTH, CV_WIDTH), 0.02),
        "w_out": nrm(ks[11], (DEPTH, D_MIX, D_MODEL), D_MIX ** -0.5),
        "ffn_norm_w": 1.0 + nrm(ks[12], (DEPTH, D_MODEL), 0.02),
        "ffn_w1": nrm(ks[13], (N_DENSE, D_MODEL, D_FF), D_MODEL ** -0.5),
        "ffn_w3": nrm(ks[14], (N_DENSE, D_MODEL, D_FF), D_MODEL ** -0.5),
        "ffn_w2": nrm(ks[15], (N_DENSE, D_FF, D_MODEL), D_FF ** -0.5),
        "moe_router_w": nrm(ks[16], (N_MOE, D_MODEL, N_EXPERTS), D_MODEL ** -0.5),
        "moe_router_b": nrm(ks[17], (N_MOE, N_EXPERTS), 0.01),
        "moe_w1": nrm(ks[18], (N_MOE, N_EXPERTS, D_MODEL, D_FF_EXPERT), D_MODEL ** -0.5),
        "moe_w3": nrm(ks[19], (N_MOE, N_EXPERTS, D_MODEL, D_FF_EXPERT), D_MODEL ** -0.5),
        "moe_w2": nrm(ks[20], (N_MOE, N_EXPERTS, D_FF_EXPERT, D_MODEL), D_FF_EXPERT ** -0.5),
        "final_norm_w": 1.0 + nrm(ks[21], (D_MODEL,), 0.02),
    }


def reference(x, positions, attn_norm_w, w_in, diff_lambda, diff_subln_w, rel_bias, conv_dw_w, conv_dw_b,
              conv_ln_w, conv_ln_b, w_out, ffn_norm_w, ffn_w1, ffn_w3, ffn_w2, moe_router_w, moe_router_b,
              moe_w1, moe_w3, moe_w2, final_norm_w):
    B, S, _ = x.shape
    inv_freq = ROPE_THETA ** (-jnp.arange(0, ROPE_DIM, 2, dtype=jnp.float32) / ROPE_DIM)
    ang = positions.astype(jnp.float32)[..., None] * inv_freq
    cos = jnp.cos(ang)[:, :, None, None, :].astype(x.dtype)
    sin = jnp.sin(ang)[:, :, None, None, :].astype(x.dtype)
    splits = [int(v) for v in np.cumsum(COL_SIZES)[:-1]]
    h = x
    for l in range(DEPTH):
        u = rms_norm(h, attn_norm_w[l])
        proj = u @ w_in[l]
        qa, ka, va, qb, kb, vb, cva, cvg = jnp.split(proj, splits, axis=-1)
        qa = rope_partial(qa.reshape(B, S, DA_HEADS, 2, DA_QK_DIM), cos, sin)
        ka = rope_partial(ka.reshape(B, S, DA_HEADS, 2, DA_QK_DIM), cos, sin)
        va = va.reshape(B, S, DA_HEADS, DA_V_DIM)
        lam_init = 0.8 - 0.6 * math.exp(-0.3 * l)
        lv = diff_lambda[l].astype(jnp.float32)
        lam = jnp.exp(jnp.sum(lv[0] * lv[1])) - jnp.exp(jnp.sum(lv[2] * lv[3])) + lam_init
        oa = diff_attention(qa, ka, va, lam, lam_init, diff_subln_w[l])
        ob = chunk_band_attention(qb.reshape(B, S, CB_HEADS, CB_HEAD_DIM), kb.reshape(B, S, CB_HEADS, CB_HEAD_DIM),
                                  vb.reshape(B, S, CB_HEADS, CB_HEAD_DIM), rel_bias[l])
        oc = conv_module(cva, cvg, conv_dw_w[l], conv_dw_b[l], conv_ln_w[l], conv_ln_b[l])
        h = h + jnp.concatenate([oa, ob, oc], axis=-1) @ w_out[l]
        u = rms_norm(h, ffn_norm_w[l])
        if l % 2 == 0:
            i = l // 2
            h = h + swiglu(u, ffn_w1[i], ffn_w3[i], ffn_w2[i])
        else:
            i = l // 2
            h = h + moe_swiglu(u, moe_router_w[i], moe_router_b[i], moe_w1[i], moe_w3[i], moe_w2[i])
    return rms_norm(h, final_norm_w)
```

```python
import functools
import math

import jax
import jax.numpy as jnp
from jax import lax
from jax.experimental import pallas as pl
from jax.experimental.pallas import tpu as pltpu

F32 = jnp.float32
BF16 = jnp.bfloat16

CHUNK = 64
DA_QK_DIM = 64
DA_V_DIM = 2 * DA_QK_DIM
CB_HEAD_DIM = 128
CB_LEFT_CHUNKS = 8
REL_CLIP = 128
CONV_WIDTH = 31
ROPE_THETA = 500000.0
ROPE_DIM = DA_QK_DIM // 4
TOP_K = 2
NORM_EPS = 1e-6
NEG = -1e30

LANES = 128
SUBLANES = 8
V7X_VMEM_LIMIT_BYTES = 60000 * 1024

CONV_HALO = 32


def _params(*sem):
    return pltpu.CompilerParams(dimension_semantics=sem, vmem_limit_bytes=V7X_VMEM_LIMIT_BYTES)


def _tile(n, target):
    if n <= target:
        return n
    t = target
    while n % t:
        t -= 8
    return t


def _sigmoid(x):
    return 1.0 / (1.0 + jnp.exp(-x))


def _rms(x, w):
    return x * lax.rsqrt(jnp.mean(x * x, axis=-1, keepdims=True) + NORM_EPS) * w


def _rmsnorm_kernel(x_ref, w_ref, o_ref):
    o_ref[...] = _rms(x_ref[...], w_ref[...]).astype(o_ref.dtype)


def rmsnorm(x, w, out_dtype):
    t, d = x.shape
    tm = _tile(t, 1024)
    return pl.pallas_call(
        _rmsnorm_kernel,
        out_shape=jax.ShapeDtypeStruct((t, d), out_dtype),
        grid=(t // tm,),
        in_specs=[pl.BlockSpec((tm, d), lambda i: (i, 0)), pl.BlockSpec((1, d), lambda i: (0, 0))],
        out_specs=pl.BlockSpec((tm, d), lambda i: (i, 0)),
        compiler_params=_params("parallel"),
        name="rmsnorm",
    )(x, w.reshape(1, d))


def _rope_tile(acc, c, s1, s2):
    half = ROPE_DIM // 2
    parts = []
    for g in range(acc.shape[1] // LANES):
        x = acc[:, g * LANES:(g + 1) * LANES]
        parts.append(x * c + pltpu.roll(x, LANES - half, 1) * s1 + pltpu.roll(x, half, 1) * s2)
    return jnp.concatenate(parts, axis=1)


def _inproj_kernel(u_ref, w_ref, c_ref, s1_ref, s2_ref, o_ref, *, qa_tiles, ka_tiles, qb_lo, qb_hi,
                   qa_scale, qb_scale):
    j = pl.program_id(1)
    acc = jnp.dot(u_ref[...], w_ref[...].astype(BF16), preferred_element_type=F32)

    @pl.when(j < qa_tiles)
    def _():
        o_ref[...] = (_rope_tile(acc, c_ref[...], s1_ref[...], s2_ref[...]) * qa_scale).astype(o_ref.dtype)

    @pl.when((j >= qa_tiles) & (j < qa_tiles + ka_tiles))
    def _():
        o_ref[...] = _rope_tile(acc, c_ref[...], s1_ref[...], s2_ref[...]).astype(o_ref.dtype)

    @pl.when((j >= qb_lo) & (j < qb_hi))
    def _():
        o_ref[...] = (acc * qb_scale).astype(o_ref.dtype)

    @pl.when((j >= qa_tiles + ka_tiles) & ((j < qb_lo) | (j >= qb_hi)))
    def _():
        o_ref[...] = acc.astype(o_ref.dtype)


def in_projection(u, w, rope_c, rope_s1, rope_s2, dims):
    t, d = u.shape
    n = w.shape[1]
    tm = _tile(t, 1024)
    tn = 512
    assert n % tn == 0 and dims["qa"] % tn == 0 and dims["ka"] % tn == 0 and dims["qb_lo"] % tn == 0 \
        and dims["qb_hi"] % tn == 0
    kern = functools.partial(
        _inproj_kernel, qa_tiles=dims["qa"] // tn, ka_tiles=dims["ka"] // tn, qb_lo=dims["qb_lo"] // tn,
        qb_hi=dims["qb_hi"] // tn, qa_scale=DA_QK_DIM ** -0.5, qb_scale=CB_HEAD_DIM ** -0.5)
    tab = pl.BlockSpec((tm, LANES), lambda i, j: (i, 0))
    return pl.pallas_call(
        kern,
        out_shape=jax.ShapeDtypeStruct((t, n), BF16),
        grid=(t // tm, n // tn),
        in_specs=[pl.BlockSpec((tm, d), lambda i, j: (i, 0)), pl.BlockSpec((d, tn), lambda i, j: (0, j)),
                  tab, tab, tab],
        out_specs=pl.BlockSpec((tm, tn), lambda i, j: (i, j)),
        compiler_params=_params("parallel", "arbitrary"),
        name="in_projection",
    )(u, w, rope_c, rope_s1, rope_s2)


def _diffattn_kernel(q_ref, k_ref, v_ref, lam_ref, sw_ref, o_ref, qc_ref, m_ref, l_ref, acc_ref, *, tq,
                     lam_init):
    i = pl.program_id(2)
    q = q_ref[...]
    lane = lax.broadcasted_iota(jnp.int32, q.shape, 1)
    zero = jnp.zeros_like(q)
    qc_ref[0:tq, :] = jnp.where(lane < DA_QK_DIM, q, zero)
    qc_ref[tq:2 * tq, :] = jnp.where(lane >= DA_QK_DIM, q, zero)
    m_ref[...] = jnp.full(m_ref.shape, NEG, F32)
    l_ref[...] = jnp.zeros(l_ref.shape, F32)
    acc_ref[...] = jnp.zeros(acc_ref.shape, F32)

    def step(j, masked):
        start = pl.multiple_of(j * tq, tq)
        k = k_ref[pl.ds(start, tq), :]
        v = v_ref[pl.ds(start, tq), :]
        s = lax.dot_general(k, qc_ref[...], (((1,), (1,)), ((), ())), preferred_element_type=F32)
        if masked:
            kc = lax.broadcasted_iota(jnp.int32, s.shape, 0) // CHUNK
            qi = lax.broadcasted_iota(jnp.int32, s.shape, 1)
            qcx = jnp.where(qi >= tq, qi - tq, qi) // CHUNK
            s = jnp.where(kc <= qcx, s, NEG)
        m_prev = m_ref[...]
        m_new = jnp.maximum(m_prev, jnp.max(s, axis=0, keepdims=True))
        alpha = jnp.exp(m_prev - m_new)
        p = jnp.exp(s - m_new)
        l_ref[...] = alpha * l_ref[...] + jnp.sum(p, axis=0, keepdims=True)
        pv = lax.dot_general(v, p.astype(BF16), (((0,), (0,)), ((), ())), preferred_element_type=F32)
        acc_ref[...] = alpha * acc_ref[...] + pv
        m_ref[...] = m_new

    def body(j, carry):
        step(j, False)
        return carry

    lax.fori_loop(0, i, body, 0)
    step(i, True)

    lv = lam_ref[...]
    lam = (jnp.exp(jnp.sum(lv[0:1] * lv[1:2], keepdims=True))
           - jnp.exp(jnp.sum(lv[2:3] * lv[3:4], keepdims=True)) + lam_init)
    o_t = acc_ref[...] / l_ref[...]
    o = (o_t[:, 0:tq] - lam * o_t[:, tq:2 * tq]).T
    o_ref[...] = (_rms(o, sw_ref[...]) * (1.0 - lam_init)).astype(o_ref.dtype)


def diff_attention(proj, lam_vec, subln_w, lam_init, dims, bsz, seq):
    heads = dims["da_heads"]
    tq = _tile(seq, 512)
    p3 = proj.reshape(bsz, seq, proj.shape[-1])
    kcol = dims["qa"] // LANES
    vcol = (dims["qa"] + dims["ka"]) // LANES
    kern = functools.partial(_diffattn_kernel, tq=tq, lam_init=lam_init)
    out = pl.pallas_call(
        kern,
        out_shape=jax.ShapeDtypeStruct((bsz, seq, heads * DA_V_DIM), BF16),
        grid=(bsz, heads, seq // tq),
        in_specs=[
            pl.BlockSpec((None, tq, LANES), lambda b, h, i: (b, i, h)),
            pl.BlockSpec((None, seq, LANES), lambda b, h, i: (b, 0, kcol + h)),
            pl.BlockSpec((None, seq, LANES), lambda b, h, i: (b, 0, vcol + h)),
            pl.BlockSpec(lam_vec.shape, lambda b, h, i: (0, 0)),
            pl.BlockSpec((1, DA_V_DIM), lambda b, h, i: (0, 0)),
        ],
        out_specs=pl.BlockSpec((None, tq, DA_V_DIM), lambda b, h, i: (b, i, h)),
        scratch_shapes=[pltpu.VMEM((2 * tq, LANES), BF16), pltpu.VMEM((1, 2 * tq), F32),
                        pltpu.VMEM((1, 2 * tq), F32), pltpu.VMEM((DA_V_DIM, 2 * tq), F32)],
        compiler_params=_params("parallel", "parallel", "arbitrary"),
        name="diff_attention",
    )(p3, p3, p3, lam_vec, subln_w.reshape(1, DA_V_DIM))
    return out.reshape(bsz * seq, heads * DA_V_DIM)


def _bandattn_kernel(q_ref, kp_ref, kc_ref, vp_ref, vc_ref, bias_ref, o_ref, *, tq):
    i = pl.program_id(2)
    k = jnp.concatenate([kp_ref[...], kc_ref[...]], axis=0)
    v = jnp.concatenate([vp_ref[...], vc_ref[...]], axis=0)
    s = lax.dot_general(q_ref[...], k, (((1,), (1,)), ((), ())), preferred_element_type=F32) + bias_ref[...]
    col = lax.broadcasted_iota(jnp.int32, s.shape, 1)
    s = jnp.where((i == 0) & (col < tq), NEG, s)
    p = jnp.exp(s - jnp.max(s, axis=-1, keepdims=True))
    p = p / jnp.sum(p, axis=-1, keepdims=True)
    o_ref[...] = jnp.dot(p.astype(BF16), v, preferred_element_type=F32).astype(o_ref.dtype)


def band_bias(rel_bias, tq):
    qi = jnp.arange(tq)[:, None]
    kj = jnp.arange(2 * tq)[None, :]
    idx = jnp.clip(qi + tq - kj, -REL_CLIP, REL_CLIP) + REL_CLIP
    qc, kc = qi // CHUNK, kj // CHUNK
    visible = (kc >= qc) & (kc <= qc + CB_LEFT_CHUNKS)
    return jnp.where(visible[None], rel_bias[:, idx].astype(F32), NEG)


def band_attention(proj, rel_bias, dims, bsz, seq):
    heads = dims["cb_heads"]
    tq = CB_LEFT_CHUNKS * CHUNK
    assert seq % tq == 0
    p3 = proj.reshape(bsz, seq, proj.shape[-1])
    qcol = dims["qb_lo"] // LANES
    kcol = qcol + heads
    vcol = kcol + heads
    bias = band_bias(rel_bias, tq)
    prev = lambda col: pl.BlockSpec((None, tq, LANES), lambda b, h, i: (b, jnp.maximum(i - 1, 0), col + h))
    cur = lambda col: pl.BlockSpec((None, tq, LANES), lambda b, h, i: (b, i, col + h))
    out = pl.pallas_call(
        functools.partial(_bandattn_kernel, tq=tq),
        out_shape=jax.ShapeDtypeStruct((bsz, seq, heads * CB_HEAD_DIM), BF16),
        grid=(bsz, heads, seq // tq),
        in_specs=[cur(qcol), prev(kcol), cur(kcol), prev(vcol), cur(vcol),
                  pl.BlockSpec((None, tq, 2 * tq), lambda b, h, i: (h, 0, 0))],
        out_specs=pl.BlockSpec((None, tq, CB_HEAD_DIM), lambda b, h, i: (b, i, h)),
        compiler_params=_params("parallel", "parallel", "arbitrary"),
        name="band_attention",
    )(p3, p3, p3, p3, p3, bias)
    return out.reshape(bsz * seq, heads * CB_HEAD_DIM)


def _conv_kernel(a_ref, g_ref, ap_ref, gp_ref, w_ref, b_ref, lnw_ref, lnb_ref, o_ref, u_ref, *, tc, sub):
    i = pl.program_id(1)
    up = ap_ref[...].astype(F32) * _sigmoid(gp_ref[...].astype(F32))
    u_ref[0, 0:CONV_HALO, :] = jnp.where(i > 0, up, 0.0)
    u_ref[0, CONV_HALO:CONV_HALO + tc, :] = a_ref[...].astype(F32) * _sigmoid(g_ref[...].astype(F32))
    n_shift = CONV_HALO + tc - SUBLANES
    for s in range(1, SUBLANES):
        u_ref[s, 0:n_shift, :] = u_ref[0, s:s + n_shift, :]
    first = CONV_HALO - (CONV_WIDTH - 1)

    def body(r, carry):
        r0 = pl.multiple_of(r * sub, sub)
        acc = jnp.zeros((sub, u_ref.shape[2]), F32) + b_ref[...]
        for j in range(CONV_WIDTH):
            off = first + j
            acc = acc + w_ref[j:j + 1, :] * u_ref[off % SUBLANES, pl.ds(r0 + off - off % SUBLANES, sub), :]
        xc = acc - jnp.mean(acc, axis=-1, keepdims=True)
        y = xc * lax.rsqrt(jnp.mean(xc * xc, axis=-1, keepdims=True) + NORM_EPS) * lnw_ref[...] + lnb_ref[...]
        o_ref[pl.ds(r0, sub), :] = (y * _sigmoid(y)).astype(o_ref.dtype)
        return carry

    lax.fori_loop(0, tc // sub, body, 0)


def conv_module(proj, dw_w, dw_b, ln_w, ln_b, dims, bsz, seq):
    c = dims["cv"]
    tc = _tile(seq, 512)
    sub = 32
    p3 = proj.reshape(bsz, seq, proj.shape[-1])
    acol = dims["cv_lo"] // c
    gcol = acol + 1
    per_blk = tc // CONV_HALO
    cur = lambda col: pl.BlockSpec((None, tc, c), lambda b, i: (b, i, col))
    prev = lambda col: pl.BlockSpec((None, CONV_HALO, c), lambda b, i: (b, jnp.maximum(i * per_blk - 1, 0), col))
    row = pl.BlockSpec((1, c), lambda b, i: (0, 0))
    out = pl.pallas_call(
        functools.partial(_conv_kernel, tc=tc, sub=sub),
        out_shape=jax.ShapeDtypeStruct((bsz, seq, c), BF16),
        grid=(bsz, seq // tc),
        in_specs=[cur(acol), cur(gcol), prev(acol), prev(gcol),
                  pl.BlockSpec((CONV_WIDTH, c), lambda b, i: (0, 0)), row, row, row],
        out_specs=pl.BlockSpec((None, tc, c), lambda b, i: (b, i, 0)),
        scratch_shapes=[pltpu.VMEM((SUBLANES, CONV_HALO + tc, c), F32)],
        compiler_params=_params("parallel", "arbitrary"),
        name="conv_module",
    )(p3, p3, p3, p3, dw_w, dw_b.reshape(1, c), ln_w.reshape(1, c), ln_b.reshape(1, c))
    return out.reshape(bsz * seq, c)


def _outproj_kernel(h_ref, oa_ref, ob_ref, oc_ref, w_ref, nw_ref, hn_ref, u_ref, *, ka, kb):
    acc = h_ref[...]
    acc = acc + jnp.dot(oa_ref[...], w_ref[0:ka, :].astype(BF16), preferred_element_type=F32)
    acc = acc + jnp.dot(ob_ref[...], w_ref[ka:ka + kb, :].astype(BF16), preferred_element_type=F32)
    acc = acc + jnp.dot(oc_ref[...], w_ref[ka + kb:, :].astype(BF16), preferred_element_type=F32)
    hn_ref[...] = acc
    u_ref[...] = _rms(acc, nw_ref[...]).astype(u_ref.dtype)


def out_projection(h, oa, ob, oc, w, norm_w, u_dtype):
    t, d = h.shape
    tm = _tile(t, 512)
    ka, kb, kc = oa.shape[1], ob.shape[1], oc.shape[1]
    rows = lambda k: pl.BlockSpec((tm, k), lambda i: (i, 0))
    return pl.pallas_call(
        functools.partial(_outproj_kernel, ka=ka, kb=kb),
        out_shape=(jax.ShapeDtypeStruct((t, d), F32), jax.ShapeDtypeStruct((t, d), u_dtype)),
        grid=(t // tm,),
        in_specs=[rows(d), rows(ka), rows(kb), rows(kc), pl.BlockSpec(w.shape, lambda i: (0, 0)),
                  pl.BlockSpec((1, d), lambda i: (0, 0))],
        out_specs=(rows(d), rows(d)),
        compiler_params=_params("parallel"),
        name="out_projection",
    )(h, oa, ob, oc, w, norm_w.reshape(1, d))


def _swiglu_partial(x, w1_ref, w3_ref, w2_ref):
    h1 = jnp.dot(x, w1_ref[...].astype(BF16), preferred_element_type=F32)
    h3 = jnp.dot(x, w3_ref[...].astype(BF16), preferred_element_type=F32)
    g = (h1 * _sigmoid(h1) * h3).astype(BF16)
    return jnp.dot(g, w2_ref[...].astype(BF16), preferred_element_type=F32)


def _ffn_dense_kernel(x_ref, h_ref, w1_ref, w3_ref, w2_ref, nw_ref, hn_ref, u_ref, acc_ref):
    f = pl.program_id(1)
    part = _swiglu_partial(x_ref[...], w1_ref, w3_ref, w2_ref)

    @pl.when(f == 0)
    def _():
        acc_ref[...] = h_ref[...] + part

    @pl.when(f > 0)
    def _():
        acc_ref[...] += part

    @pl.when(f == pl.num_programs(1) - 1)
    def _():
        hn = acc_ref[...]
        hn_ref[...] = hn
        u_ref[...] = _rms(hn, nw_ref[...]).astype(u_ref.dtype)


def ffn_dense(u, h, w1, w3, w2, norm_w, u_dtype):
    t, d = u.shape
    dff = w1.shape[1]
    tm = _tile(t, 512)
    tf = _tile(dff, 512)
    rows = pl.BlockSpec((tm, d), lambda i, f: (i, 0))
    return pl.pallas_call(
        _ffn_dense_kernel,
        out_shape=(jax.ShapeDtypeStruct((t, d), F32), jax.ShapeDtypeStruct((t, d), u_dtype)),
        grid=(t // tm, dff // tf),
        in_specs=[rows, rows, pl.BlockSpec((d, tf), lambda i, f: (0, f)), pl.BlockSpec((d, tf), lambda i, f: (0, f)),
                  pl.BlockSpec((tf, d), lambda i, f: (f, 0)), pl.BlockSpec((1, d), lambda i, f: (0, 0))],
        out_specs=(rows, rows),
        scratch_shapes=[pltpu.VMEM((tm, d), F32)],
        compiler_params=_params("parallel", "arbitrary"),
        name="ffn_dense",
    )(u, h, w1, w3, w2, norm_w.reshape(1, d))


def _ffn_grouped_kernel(te_ref, nv_ref, x_ref, w1_ref, w3_ref, w2_ref, y_ref, acc_ref):
    i = pl.program_id(0)
    f = pl.program_id(1)

    @pl.when(i < nv_ref[0])
    def _():
        part = _swiglu_partial(x_ref[...].astype(BF16), w1_ref, w3_ref, w2_ref)

        @pl.when(f == 0)
        def _():
            acc_ref[...] = part

        @pl.when(f > 0)
        def _():
            acc_ref[...] += part

        @pl.when(f == pl.num_programs(1) - 1)
        def _():
            y_ref[...] = acc_ref[...]

    @pl.when((i >= nv_ref[0]) & (f == 0))
    def _():
        y_ref[...] = jnp.zeros(y_ref.shape, y_ref.dtype)


def ffn_grouped(xs, tile_expert, n_valid, w1, w3, w2, tm):
    p, d = xs.shape
    dff = w1.shape[2]
    tf = _tile(dff, 512)
    nf = dff // tf
    fidx = lambda i, f, nv: jnp.where(i < nv[0], f, nf - 1)
    grid_spec = pltpu.PrefetchScalarGridSpec(
        num_scalar_prefetch=2,
        grid=(p // tm, nf),
        in_specs=[
            pl.BlockSpec((tm, d), lambda i, f, te, nv: (jnp.minimum(i, nv[0] - 1), 0)),
            pl.BlockSpec((None, d, tf), lambda i, f, te, nv: (te[i], 0, fidx(i, f, nv))),
            pl.BlockSpec((None, d, tf), lambda i, f, te, nv: (te[i], 0, fidx(i, f, nv))),
            pl.BlockSpec((None, tf, d), lambda i, f, te, nv: (te[i], fidx(i, f, nv), 0)),
        ],
        out_specs=pl.BlockSpec((tm, d), lambda i, f, te, nv: (i, 0)),
        scratch_shapes=[pltpu.VMEM((tm, d), F32)],
    )
    return pl.pallas_call(
        _ffn_grouped_kernel,
        out_shape=jax.ShapeDtypeStruct((p, d), F32),
        grid_spec=grid_spec,
        compiler_params=_params("arbitrary", "arbitrary"),
        name="ffn_grouped",
    )(tile_expert, n_valid, xs, w1, w3, w2)


ROUTE_E1, ROUTE_E2, ROUTE_R1, ROUTE_R2, ROUTE_G1, ROUTE_G2 = range(6)


def _router_kernel(u_ref, rw_ref, rb_ref, info_ref, cnt_ref, carry_ref, *, n_exp):
    @pl.when(pl.program_id(0) == 0)
    def _():
        carry_ref[...] = jnp.zeros(carry_ref.shape, F32)

    logits = jnp.dot(u_ref[...].astype(BF16), rw_ref[...].astype(BF16), preferred_element_type=F32) + rb_ref[...]
    tm = logits.shape[0]
    lane = lax.broadcasted_iota(jnp.int32, logits.shape, 1).astype(F32)
    logits = jnp.where(lane < n_exp, logits, -jnp.inf)
    m1 = jnp.max(logits, axis=-1, keepdims=True)
    i1 = jnp.min(jnp.where(logits == m1, lane, float(LANES)), axis=-1, keepdims=True)
    rest = jnp.where(lane == i1, -jnp.inf, logits)
    m2 = jnp.max(rest, axis=-1, keepdims=True)
    i2 = jnp.min(jnp.where(rest == m2, lane, float(LANES)), axis=-1, keepdims=True)
    e = jnp.exp(m2 - m1)
    g1 = 1.0 / (1.0 + e)
    g2 = e / (1.0 + e)
    oh1 = (lane == i1).astype(F32)
    oh2 = (lane == i2).astype(F32)
    oh = oh1 + oh2
    r = lax.broadcasted_iota(jnp.int32, (tm, tm), 0)
    c = lax.broadcasted_iota(jnp.int32, (tm, tm), 1)
    before = jnp.dot((c < r).astype(BF16), oh.astype(BF16), preferred_element_type=F32) + carry_ref[...]
    r1 = jnp.sum(before * oh1, axis=-1, keepdims=True)
    r2 = jnp.sum(before * oh2, axis=-1, keepdims=True)
    carry_ref[...] += jnp.sum(oh, axis=0, keepdims=True)
    info = jnp.zeros(logits.shape, F32)
    for slot, val in ((ROUTE_E1, i1), (ROUTE_E2, i2), (ROUTE_R1, r1), (ROUTE_R2, r2), (ROUTE_G1, g1),
                      (ROUTE_G2, g2)):
        info = jnp.where(lane == slot, val, info)
    info_ref[...] = info
    cnt_ref[...] = carry_ref[...]


def router(u, rw, rb):
    t, d = u.shape
    n_exp = rw.shape[1]
    tm = _tile(t, 512)
    rw_p = jnp.zeros((d, LANES), rw.dtype).at[:, :n_exp].set(rw)
    rb_p = jnp.zeros((1, LANES), F32).at[0, :n_exp].set(rb.astype(F32))
    return pl.pallas_call(
        functools.partial(_router_kernel, n_exp=n_exp),
        out_shape=(jax.ShapeDtypeStruct((t, LANES), F32), jax.ShapeDtypeStruct((1, LANES), F32)),
        grid=(t // tm,),
        in_specs=[pl.BlockSpec((tm, d), lambda i: (i, 0)), pl.BlockSpec((d, LANES), lambda i: (0, 0)),
                  pl.BlockSpec((1, LANES), lambda i: (0, 0))],
        out_specs=(pl.BlockSpec((tm, LANES), lambda i: (i, 0)), pl.BlockSpec((1, LANES), lambda i: (0, 0))),
        scratch_shapes=[pltpu.VMEM((1, LANES), F32)],
        compiler_params=_params("arbitrary"),
        name="router",
    )(u, rw_p, rb_p)


def _row_copy(src, si, dst, di, sem):
    return pltpu.make_async_copy(src.at[pl.ds(si, 1)], dst.at[pl.ds(di, 1)], sem)


def _dispatch_kernel(dest_ref, u_ref, xs_in_ref, xs_ref, sem, *, td):
    del xs_in_ref

    def start(t, carry):
        for k in range(TOP_K):
            _row_copy(u_ref, t, xs_ref, dest_ref[k, t], sem).start()
        return carry

    def wait(t, carry):
        for k in range(TOP_K):
            _row_copy(u_ref, t, xs_ref, dest_ref[k, t], sem).wait()
        return carry

    lax.fori_loop(0, td, start, 0)
    lax.fori_loop(0, td, wait, 0)


def dispatch(u, dest, xs_zero):
    t, d = u.shape
    td = dest.shape[-1]
    return pl.pallas_call(
        functools.partial(_dispatch_kernel, td=td),
        out_shape=jax.ShapeDtypeStruct(xs_zero.shape, xs_zero.dtype),
        grid=(t // td,),
        in_specs=[pl.BlockSpec((None, TOP_K, td), lambda i: (i, 0, 0), memory_space=pltpu.SMEM),
                  pl.BlockSpec((td, d), lambda i: (i, 0)),
                  pl.BlockSpec(memory_space=pl.ANY)],
        out_specs=pl.BlockSpec(memory_space=pl.ANY),
        scratch_shapes=[pltpu.SemaphoreType.DMA(())],
        input_output_aliases={2: 0},
        compiler_params=_params("arbitrary"),
        name="moe_dispatch",
    )(dest, u, xs_zero)


def _combine_kernel(dest_ref, h_ref, info_ref, ys_ref, nw_ref, hn_ref, u_ref, buf_ref, sem, *, td):
    def start(t, carry):
        for k in range(TOP_K):
            _row_copy(ys_ref, dest_ref[k, t], buf_ref.at[k], t, sem).start()
        return carry

    def wait(t, carry):
        for k in range(TOP_K):
            _row_copy(ys_ref, dest_ref[k, t], buf_ref.at[k], t, sem).wait()
        return carry

    lax.fori_loop(0, td, start, 0)
    lax.fori_loop(0, td, wait, 0)
    info = info_ref[...]
    g1 = info[:, ROUTE_G1:ROUTE_G1 + 1]
    g2 = info[:, ROUTE_G2:ROUTE_G2 + 1]
    hn = h_ref[...] + g1 * buf_ref[0] + g2 * buf_ref[1]
    hn_ref[...] = hn
    u_ref[...] = _rms(hn, nw_ref[...]).astype(u_ref.dtype)


def combine(h, info, dest, ys, norm_w, u_dtype):
    t, d = h.shape
    td = dest.shape[-1]
    rows = pl.BlockSpec((td, d), lambda i: (i, 0))
    return pl.pallas_call(
        functools.partial(_combine_kernel, td=td),
        out_shape=(jax.ShapeDtypeStruct((t, d), F32), jax.ShapeDtypeStruct((t, d), u_dtype)),
        grid=(t // td,),
        in_specs=[pl.BlockSpec((None, TOP_K, td), lambda i: (i, 0, 0), memory_space=pltpu.SMEM),
                  rows, pl.BlockSpec((td, LANES), lambda i: (i, 0)), pl.BlockSpec(memory_space=pl.ANY),
                  pl.BlockSpec((1, d), lambda i: (0, 0))],
        out_specs=(rows, rows),
        scratch_shapes=[pltpu.VMEM((TOP_K, td, d), F32), pltpu.SemaphoreType.DMA(())],
        compiler_params=_params("arbitrary"),
        name="moe_combine",
    )(dest, h, info, ys, norm_w.reshape(1, d))


MOE_ROW_TILE = 512
MOE_TOKEN_TILE = 256


def moe_layer(u, h, rw, rb, w1, w3, w2, norm_w, u_dtype):
    t, d = u.shape
    n_exp = rw.shape[1]
    tm = MOE_ROW_TILE
    td = _tile(t, MOE_TOKEN_TILE)
    info, cnt = router(u, rw, rb)
    counts = cnt[0, :n_exp].astype(jnp.int32)
    padded = ((counts + tm - 1) // tm) * tm
    ends = jnp.cumsum(padded)
    starts = ends - padded
    e = info[:, ROUTE_E1:ROUTE_E2 + 1].astype(jnp.int32)
    rank = info[:, ROUTE_R1:ROUTE_R2 + 1].astype(jnp.int32)
    dest = (starts[e] + rank).T.reshape(TOP_K, t // td, td).transpose(1, 0, 2)
    n_rows = TOP_K * t + n_exp * tm
    n_tiles = n_rows // tm
    n_valid = (ends[-1] // tm).astype(jnp.int32)
    tile_start = jnp.minimum(jnp.arange(n_tiles, dtype=jnp.int32), n_valid - 1) * tm
    tile_expert = jnp.minimum(jnp.sum(tile_start[:, None] >= ends[None, :], axis=1), n_exp - 1).astype(jnp.int32)
    xs = dispatch(u, dest, jnp.zeros((n_rows, d), u.dtype))
    ys = ffn_grouped(xs, tile_expert, n_valid.reshape(1), w1, w3, w2, tm)
    return combine(h, info, dest, ys, norm_w, u_dtype)


def rope_tables(positions):
    half = ROPE_DIM // 2
    inv_freq = ROPE_THETA ** (-jnp.arange(0, ROPE_DIM, 2, dtype=F32) / ROPE_DIM)
    ang = positions.astype(F32).reshape(-1)[:, None] * inv_freq
    cos, sin = jnp.cos(ang), jnp.sin(ang)
    t = ang.shape[0]
    pad = jnp.zeros((t, DA_QK_DIM - ROPE_DIM), F32)
    c = jnp.concatenate([cos, cos, pad + 1.0], axis=1)
    s1 = jnp.concatenate([-sin, jnp.zeros_like(sin), pad], axis=1)
    s2 = jnp.concatenate([jnp.zeros_like(sin), sin, pad], axis=1)
    rep = lambda a: jnp.concatenate([a] * (LANES // DA_QK_DIM), axis=1)
    return rep(c), rep(s1), rep(s2)


def kernel(x, positions, attn_norm_w, w_in, diff_lambda, diff_subln_w, rel_bias, conv_dw_w, conv_dw_b, conv_ln_w,
           conv_ln_b, w_out, ffn_norm_w, ffn_w1, ffn_w3, ffn_w2, moe_router_w, moe_router_b, moe_w1, moe_w3, moe_w2,
           final_norm_w):
    bsz, seq, d = x.shape
    depth = w_in.shape[0]
    t = bsz * seq
    da_width, cb_width = d // 2, d // 4
    cv = d - da_width - cb_width
    da_heads = da_width // DA_V_DIM
    qk_cols = da_heads * 2 * DA_QK_DIM
    dims = dict(qa=qk_cols, ka=qk_cols, da_heads=da_heads, qb_lo=2 * qk_cols + da_width,
                qb_hi=2 * qk_cols + da_width + cb_width, cb_heads=cb_width // CB_HEAD_DIM,
                cv_lo=2 * qk_cols + da_width + 3 * cb_width, cv=cv)
    rope_c, rope_s1, rope_s2 = rope_tables(positions)
    h = x.reshape(t, d)
    u = rmsnorm(h, attn_norm_w[0], BF16)
    for l in range(depth):
        last = l == depth - 1
        moe = l % 2 == 1
        proj = in_projection(u, w_in[l].astype(BF16), rope_c, rope_s1, rope_s2, dims)
        lam_init = 0.8 - 0.6 * math.exp(-0.3 * l)
        oa = diff_attention(proj, diff_lambda[l], diff_subln_w[l], lam_init, dims, bsz, seq)
        ob = band_attention(proj, rel_bias[l], dims, bsz, seq)
        oc = conv_module(proj, conv_dw_w[l], conv_dw_b[l], conv_ln_w[l], conv_ln_b[l], dims, bsz, seq)
        h, u = out_projection(h, oa, ob, oc, w_out[l].astype(BF16), ffn_norm_w[l], F32 if moe else BF16)
        next_w = final_norm_w if last else attn_norm_w[(l + 1) % depth]
        next_dtype = F32 if last else BF16
        i = l // 2
        if moe:
            h, u = moe_layer(u, h, moe_router_w[i], moe_router_b[i], moe_w1[i].astype(BF16), moe_w3[i].astype(BF16),
                             moe_w2[i].astype(BF16), next_w, next_dtype)
        else:
            h, u = ffn_dense(u, h, ffn_w1[i].astype(BF16), ffn_w3[i].astype(BF16), ffn_w2[i].astype(BF16), next_w,
                             next_dtype)
    return u.reshape(bsz, seq, d)
```

```python
import functools
import math

import jax
import jax.numpy as jnp
import numpy as np
from jax import lax
from jax.experimental import pallas as pl
from jax.experimental.pallas import tpu as pltpu

F32 = jnp.float32
BF16 = jnp.bfloat16

CHUNK = 64
DA_QK_DIM = 64
DA_V_DIM = 2 * DA_QK_DIM
CB_HEAD_DIM = 128
CB_LEFT_CHUNKS = 8
REL_CLIP = 128
CONV_WIDTH = 31
ROPE_THETA = 500000.0
ROPE_DIM = DA_QK_DIM // 4
TOP_K = 2
NORM_EPS = 1e-6
NEG = -1e30

LANES = 128
SUBLANES = 8
V7X_VMEM_LIMIT_BYTES = 60000 * 1024

CONV_HALO = 32


def _params(*sem):
    return pltpu.CompilerParams(dimension_semantics=sem, vmem_limit_bytes=V7X_VMEM_LIMIT_BYTES)


def _tile(n, target):
    if n <= target:
        return n
    t = target
    while n % t:
        t -= 8
    return t


def _sigmoid(x):
    return 1.0 / (1.0 + jnp.exp(-x))


def _rms(x, w):
    return x * lax.rsqrt(jnp.mean(x * x, axis=-1, keepdims=True) + NORM_EPS) * w


def _rmsnorm_kernel(x_ref, w_ref, o_ref):
    o_ref[...] = _rms(x_ref[...], w_ref[...]).astype(o_ref.dtype)


def rmsnorm(x, w, out_dtype):
    t, d = x.shape
    tm = _tile(t, 1024)
    return pl.pallas_call(
        _rmsnorm_kernel,
        out_shape=jax.ShapeDtypeStruct((t, d), out_dtype),
        grid=(t // tm,),
        in_specs=[pl.BlockSpec((tm, d), lambda i: (i, 0)), pl.BlockSpec((1, d), lambda i: (0, 0))],
        out_specs=pl.BlockSpec((tm, d), lambda i: (i, 0)),
        compiler_params=_params("parallel"),
        name="rmsnorm",
    )(x, w.reshape(1, d))


def _rope_tile(acc, c, s1, s2):
    half = ROPE_DIM // 2
    parts = []
    for g in range(acc.shape[1] // LANES):
        x = acc[:, g * LANES:(g + 1) * LANES]
        parts.append(x * c + pltpu.roll(x, LANES - half, 1) * s1 + pltpu.roll(x, half, 1) * s2)
    return jnp.concatenate(parts, axis=1)


def _inproj_kernel(u_ref, w_ref, c_ref, s1_ref, s2_ref, o_ref, *, qa_tiles, ka_tiles, qb_lo, qb_hi,
                   qa_scale, qb_scale):
    j = pl.program_id(1)
    acc = jnp.dot(u_ref[...], w_ref[...].astype(BF16), preferred_element_type=F32)

    @pl.when(j < qa_tiles)
    def _():
        o_ref[...] = (_rope_tile(acc, c_ref[...], s1_ref[...], s2_ref[...]) * qa_scale).astype(o_ref.dtype)

    @pl.when((j >= qa_tiles) & (j < qa_tiles + ka_tiles))
    def _():
        o_ref[...] = _rope_tile(acc, c_ref[...], s1_ref[...], s2_ref[...]).astype(o_ref.dtype)

    @pl.when((j >= qb_lo) & (j < qb_hi))
    def _():
        o_ref[...] = (acc * qb_scale).astype(o_ref.dtype)

    @pl.when((j >= qa_tiles + ka_tiles) & ((j < qb_lo) | (j >= qb_hi)))
    def _():
        o_ref[...] = acc.astype(o_ref.dtype)


def in_projection(u, w, layer, rope_c, rope_s1, rope_s2, dims):
    t, d = u.shape
    n = w.shape[2]
    tm = _tile(t, 1024)
    tn = 512
    assert n % tn == 0 and dims["qa"] % tn == 0 and dims["ka"] % tn == 0 and dims["qb_lo"] % tn == 0 \
        and dims["qb_hi"] % tn == 0
    kern = functools.partial(
        _inproj_kernel, qa_tiles=dims["qa"] // tn, ka_tiles=dims["ka"] // tn, qb_lo=dims["qb_lo"] // tn,
        qb_hi=dims["qb_hi"] // tn, qa_scale=DA_QK_DIM ** -0.5 * math.log2(math.e), qb_scale=CB_HEAD_DIM ** -0.5)
    tab = pl.BlockSpec((tm, LANES), lambda i, j: (i, 0))
    return pl.pallas_call(
        kern,
        out_shape=jax.ShapeDtypeStruct((t, n), BF16),
        grid=(t // tm, n // tn),
        in_specs=[pl.BlockSpec((tm, d), lambda i, j: (i, 0)),
                  pl.BlockSpec((None, d, tn), lambda i, j: (layer, 0, j)), tab, tab, tab],
        out_specs=pl.BlockSpec((tm, tn), lambda i, j: (i, j)),
        compiler_params=_params("parallel", "arbitrary"),
        name="in_projection",
    )(u, w, rope_c, rope_s1, rope_s2)


DA_SUM_ROWS = 16


def _diffattn_kernel(q_ref, k_ref, v_ref, lam_ref, sw_ref, o_ref, qc_ref, vt_ref, m_ref, acc_ref, *, tq, lam_init):
    i = pl.program_id(2)

    @pl.when(i == 0)
    def _():
        def transpose_block(c, carry):
            start = pl.multiple_of(c * tq, tq)
            vt_ref[c, 0:DA_V_DIM, :] = v_ref[pl.ds(start, tq), :].astype(F32).T.astype(BF16)
            vt_ref[c, DA_V_DIM:, :] = jnp.ones((DA_SUM_ROWS, tq), BF16)
            return carry

        lax.fori_loop(0, vt_ref.shape[0], transpose_block, 0)

    q = q_ref[...]
    lane = lax.broadcasted_iota(jnp.int32, q.shape, 1)
    zero = jnp.zeros_like(q)
    qc_ref[0:tq, :] = jnp.where(lane < DA_QK_DIM, q, zero)
    qc_ref[tq:2 * tq, :] = jnp.where(lane >= DA_QK_DIM, q, zero)
    m_ref[...] = jnp.full(m_ref.shape, NEG, F32)
    acc_ref[...] = jnp.zeros(acc_ref.shape, F32)

    def scores(j, masked):
        k = k_ref[pl.ds(pl.multiple_of(j * tq, tq), tq), :]
        s = lax.dot_general(k, qc_ref[...], (((1,), (1,)), ((), ())), preferred_element_type=F32)
        if masked:
            kc = lax.broadcasted_iota(jnp.int32, s.shape, 0) // CHUNK
            qi = lax.broadcasted_iota(jnp.int32, s.shape, 1)
            qcx = jnp.where(qi >= tq, qi - tq, qi) // CHUNK
            s = jnp.where(kc <= qcx, s, NEG)
        return s

    def accumulate(j, s):
        m_prev = m_ref[...]
        m_new = jnp.maximum(m_prev, jnp.max(s, axis=0, keepdims=True))
        alpha = jnp.exp2(m_prev - m_new)
        p = jnp.exp2(s - m_new).astype(BF16)
        acc_ref[...] = alpha * acc_ref[...] + jnp.dot(vt_ref[j], p, preferred_element_type=F32)
        m_ref[...] = m_new

    def pair(j0, second_masked):
        s0 = scores(j0, False)
        s1 = scores(j0 + 1, second_masked)
        accumulate(j0, s0)
        accumulate(j0 + 1, s1)

    def body(jj, carry):
        pair(2 * jj, False)
        return carry

    lax.fori_loop(0, i // 2, body, 0)

    @pl.when(i % 2 == 1)
    def _():
        pair(i - 1, True)

    @pl.when(i % 2 == 0)
    def _():
        accumulate(i, scores(i, True))

    lv = lam_ref[...]
    lam = (jnp.exp(jnp.sum(lv[0:1] * lv[1:2], keepdims=True))
           - jnp.exp(jnp.sum(lv[2:3] * lv[3:4], keepdims=True)) + lam_init)
    o_t = acc_ref[0:DA_V_DIM, :] / acc_ref[DA_V_DIM:DA_V_DIM + 1, :]
    o = (o_t[:, 0:tq] - lam * o_t[:, tq:2 * tq]).T
    o_ref[...] = (_rms(o, sw_ref[...]) * (1.0 - lam_init)).astype(o_ref.dtype)


def diff_attention(proj, lam_vec, subln_w, lam_init, dims, bsz, seq):
    heads = dims["da_heads"]
    tq = _tile(seq, 512)
    p3 = proj.reshape(bsz, seq, proj.shape[-1])
    kcol = dims["qa"] // LANES
    vcol = (dims["qa"] + dims["ka"]) // LANES
    kern = functools.partial(_diffattn_kernel, tq=tq, lam_init=lam_init)
    out = pl.pallas_call(
        kern,
        out_shape=jax.ShapeDtypeStruct((bsz, seq, heads * DA_V_DIM), BF16),
        grid=(bsz, heads, seq // tq),
        in_specs=[
            pl.BlockSpec((None, tq, LANES), lambda b, h, i: (b, i, h)),
            pl.BlockSpec((None, seq, LANES), lambda b, h, i: (b, 0, kcol + h)),
            pl.BlockSpec((None, seq, LANES), lambda b, h, i: (b, 0, vcol + h)),
            pl.BlockSpec(lam_vec.shape, lambda b, h, i: (0, 0)),
            pl.BlockSpec((1, DA_V_DIM), lambda b, h, i: (0, 0)),
        ],
        out_specs=pl.BlockSpec((None, tq, DA_V_DIM), lambda b, h, i: (b, i, h)),
        scratch_shapes=[pltpu.VMEM((2 * tq, LANES), BF16),
                        pltpu.VMEM((seq // tq, DA_V_DIM + DA_SUM_ROWS, tq), BF16),
                        pltpu.VMEM((1, 2 * tq), F32), pltpu.VMEM((DA_V_DIM + DA_SUM_ROWS, 2 * tq), F32)],
        compiler_params=_params("parallel", "parallel", "arbitrary"),
        name="diff_attention",
    )(p3, p3, p3, lam_vec, subln_w.reshape(1, DA_V_DIM))
    return out.reshape(bsz * seq, heads * DA_V_DIM)


def _bandattn_kernel(q_ref, kp_ref, kc_ref, vp_ref, vc_ref, bias_ref, o_ref, *, tq):
    i = pl.program_id(2)
    k = jnp.concatenate([kp_ref[...], kc_ref[...]], axis=0)
    v = jnp.concatenate([vp_ref[...], vc_ref[...]], axis=0)
    s = lax.dot_general(q_ref[...], k, (((1,), (1,)), ((), ())), preferred_element_type=F32) + bias_ref[...]
    col = lax.broadcasted_iota(jnp.int32, s.shape, 1)
    s = jnp.where((i == 0) & (col < tq), NEG, s)
    p = jnp.exp(s - jnp.max(s, axis=-1, keepdims=True))
    p = p / jnp.sum(p, axis=-1, keepdims=True)
    o_ref[...] = jnp.dot(p.astype(BF16), v, preferred_element_type=F32).astype(o_ref.dtype)


def band_bias(rel_bias, tq):
    assert tq > REL_CLIP
    n_h = rel_bias.shape[0]
    p = 3 * tq
    rb = rel_bias.astype(F32)
    far = jnp.broadcast_to(rb[:, 2 * REL_CLIP:], (n_h, p))
    behind = jnp.broadcast_to(rb[:, :1], (n_h, p))
    r = jnp.concatenate([far[:, :tq - REL_CLIP + 1], rb[:, 1:2 * REL_CLIP][:, ::-1], behind[:, :tq - REL_CLIP],
                         far[:, :tq]], axis=1)
    toep = jnp.tile(r, (1, tq))[:, :tq * (p - 1)].reshape(n_h, tq, p - 1)[:, :, :2 * tq]
    qc = np.arange(tq)[:, None] // CHUNK
    kc = np.arange(2 * tq)[None, :] // CHUNK
    visible = (kc >= qc) & (kc <= qc + CB_LEFT_CHUNKS)
    return jnp.where(visible[None], toep, NEG)


BAND_TQ = CB_LEFT_CHUNKS * CHUNK


def band_attention(proj, bias, layer, dims, bsz, seq):
    heads = dims["cb_heads"]
    tq = BAND_TQ
    assert seq % tq == 0
    p3 = proj.reshape(bsz, seq, proj.shape[-1])
    qcol = dims["qb_lo"] // LANES
    kcol = qcol + heads
    vcol = kcol + heads
    prev = lambda col: pl.BlockSpec((None, tq, LANES), lambda b, h, i: (b, jnp.maximum(i - 1, 0), col + h))
    cur = lambda col: pl.BlockSpec((None, tq, LANES), lambda b, h, i: (b, i, col + h))
    out = pl.pallas_call(
        functools.partial(_bandattn_kernel, tq=tq),
        out_shape=jax.ShapeDtypeStruct((bsz, seq, heads * CB_HEAD_DIM), BF16),
        grid=(bsz, heads, seq // tq),
        in_specs=[cur(qcol), prev(kcol), cur(kcol), prev(vcol), cur(vcol),
                  pl.BlockSpec((None, tq, 2 * tq), lambda b, h, i: (layer * heads + h, 0, 0))],
        out_specs=pl.BlockSpec((None, tq, CB_HEAD_DIM), lambda b, h, i: (b, i, h)),
        compiler_params=_params("parallel", "parallel", "arbitrary"),
        name="band_attention",
    )(p3, p3, p3, p3, p3, bias)
    return out.reshape(bsz * seq, heads * CB_HEAD_DIM)


def _conv_kernel(a_ref, g_ref, ap_ref, gp_ref, w_ref, b_ref, lnw_ref, lnb_ref, o_ref, u_ref, *, tc, sub):
    i = pl.program_id(1)
    up = ap_ref[...].astype(F32) * _sigmoid(gp_ref[...].astype(F32))
    u_ref[0, 0:CONV_HALO, :] = jnp.where(i > 0, up, 0.0)
    u_ref[0, CONV_HALO:CONV_HALO + tc, :] = a_ref[...].astype(F32) * _sigmoid(g_ref[...].astype(F32))
    n_shift = CONV_HALO + tc - SUBLANES
    for s in range(1, SUBLANES):
        u_ref[s, 0:n_shift, :] = u_ref[0, s:s + n_shift, :]
    first = CONV_HALO - (CONV_WIDTH - 1)

    def body(r, carry):
        r0 = pl.multiple_of(r * sub, sub)
        acc = jnp.zeros((sub, u_ref.shape[2]), F32) + b_ref[...]
        for j in range(CONV_WIDTH):
            off = first + j
            acc = acc + w_ref[j:j + 1, :] * u_ref[off % SUBLANES, pl.ds(r0 + off - off % SUBLANES, sub), :]
        xc = acc - jnp.mean(acc, axis=-1, keepdims=True)
        y = xc * lax.rsqrt(jnp.mean(xc * xc, axis=-1, keepdims=True) + NORM_EPS) * lnw_ref[...] + lnb_ref[...]
        o_ref[pl.ds(r0, sub), :] = (y * _sigmoid(y)).astype(o_ref.dtype)
        return carry

    lax.fori_loop(0, tc // sub, body, 0)


def conv_module(proj, dw_w, dw_b, ln_w, ln_b, dims, bsz, seq):
    c = dims["cv"]
    tc = _tile(seq, 512)
    sub = 32
    p3 = proj.reshape(bsz, seq, proj.shape[-1])
    acol = dims["cv_lo"] // c
    gcol = acol + 1
    per_blk = tc // CONV_HALO
    cur = lambda col: pl.BlockSpec((None, tc, c), lambda b, i: (b, i, col))
    prev = lambda col: pl.BlockSpec((None, CONV_HALO, c), lambda b, i: (b, jnp.maximum(i * per_blk - 1, 0), col))
    row = pl.BlockSpec((1, c), lambda b, i: (0, 0))
    out = pl.pallas_call(
        functools.partial(_conv_kernel, tc=tc, sub=sub),
        out_shape=jax.ShapeDtypeStruct((bsz, seq, c), BF16),
        grid=(bsz, seq // tc),
        in_specs=[cur(acol), cur(gcol), prev(acol), prev(gcol),
                  pl.BlockSpec((CONV_WIDTH, c), lambda b, i: (0, 0)), row, row, row],
        out_specs=pl.BlockSpec((None, tc, c), lambda b, i: (b, i, 0)),
        scratch_shapes=[pltpu.VMEM((SUBLANES, CONV_HALO + tc, c), F32)],
        compiler_params=_params("parallel", "arbitrary"),
        name="conv_module",
    )(p3, p3, p3, p3, dw_w, dw_b.reshape(1, c), ln_w.reshape(1, c), ln_b.reshape(1, c))
    return out.reshape(bsz * seq, c)


def _outproj_kernel(h_ref, oa_ref, ob_ref, oc_ref, w_ref, nw_ref, hn_ref, u_ref, *, ka, kb):
    acc = h_ref[...]
    acc = acc + jnp.dot(oa_ref[...], w_ref[0:ka, :].astype(BF16), preferred_element_type=F32)
    acc = acc + jnp.dot(ob_ref[...], w_ref[ka:ka + kb, :].astype(BF16), preferred_element_type=F32)
    acc = acc + jnp.dot(oc_ref[...], w_ref[ka + kb:, :].astype(BF16), preferred_element_type=F32)
    hn_ref[...] = acc
    u_ref[...] = _rms(acc, nw_ref[...]).astype(u_ref.dtype)


def out_projection(h, oa, ob, oc, w, layer, norm_w, u_dtype):
    t, d = h.shape
    tm = _tile(t, 512)
    ka, kb, kc = oa.shape[1], ob.shape[1], oc.shape[1]
    rows = lambda k: pl.BlockSpec((tm, k), lambda i: (i, 0))
    return pl.pallas_call(
        functools.partial(_outproj_kernel, ka=ka, kb=kb),
        out_shape=(jax.ShapeDtypeStruct((t, d), F32), jax.ShapeDtypeStruct((t, d), u_dtype)),
        grid=(t // tm,),
        in_specs=[rows(d), rows(ka), rows(kb), rows(kc),
                  pl.BlockSpec((None,) + w.shape[1:], lambda i: (layer, 0, 0)),
                  pl.BlockSpec((1, d), lambda i: (0, 0))],
        out_specs=(rows(d), rows(d)),
        compiler_params=_params("parallel"),
        name="out_projection",
    )(h, oa, ob, oc, w, norm_w.reshape(1, d))


def _swiglu_partial(x, w1_ref, w3_ref, w2_ref):
    h1 = jnp.dot(x, w1_ref[...].astype(BF16), preferred_element_type=F32)
    h3 = jnp.dot(x, w3_ref[...].astype(BF16), preferred_element_type=F32)
    g = (h1 * _sigmoid(h1) * h3).astype(BF16)
    return jnp.dot(g, w2_ref[...].astype(BF16), preferred_element_type=F32)


def _ffn_dense_kernel(x_ref, h_ref, w1_ref, w3_ref, w2_ref, nw_ref, hn_ref, u_ref, acc_ref):
    f = pl.program_id(1)
    part = _swiglu_partial(x_ref[...], w1_ref, w3_ref, w2_ref)

    @pl.when(f == 0)
    def _():
        acc_ref[...] = h_ref[...] + part

    @pl.when(f > 0)
    def _():
        acc_ref[...] += part

    @pl.when(f == pl.num_programs(1) - 1)
    def _():
        hn = acc_ref[...]
        hn_ref[...] = hn
        u_ref[...] = _rms(hn, nw_ref[...]).astype(u_ref.dtype)


def ffn_dense(u, h, w1, w3, w2, idx, norm_w, u_dtype):
    t, d = u.shape
    dff = w1.shape[2]
    tm = _tile(t, 512)
    tf = _tile(dff, 512)
    rows = pl.BlockSpec((tm, d), lambda i, f: (i, 0))
    w_in = pl.BlockSpec((None, d, tf), lambda i, f: (idx, 0, f))
    return pl.pallas_call(
        _ffn_dense_kernel,
        out_shape=(jax.ShapeDtypeStruct((t, d), F32), jax.ShapeDtypeStruct((t, d), u_dtype)),
        grid=(t // tm, dff // tf),
        in_specs=[rows, rows, w_in, w_in, pl.BlockSpec((None, tf, d), lambda i, f: (idx, f, 0)),
                  pl.BlockSpec((1, d), lambda i, f: (0, 0))],
        out_specs=(rows, rows),
        scratch_shapes=[pltpu.VMEM((tm, d), F32)],
        compiler_params=_params("parallel", "arbitrary"),
        name="ffn_dense",
    )(u, h, w1, w3, w2, norm_w.reshape(1, d))


def _ffn_grouped_kernel(te_ref, nv_ref, x_ref, w1_ref, w3_ref, w2_ref, y_ref, acc_ref):
    i = pl.program_id(0)
    f = pl.program_id(1)

    @pl.when(i < nv_ref[0])
    def _():
        part = _swiglu_partial(x_ref[...].astype(BF16), w1_ref, w3_ref, w2_ref)

        @pl.when(f == 0)
        def _():
            acc_ref[...] = part

        @pl.when(f > 0)
        def _():
            acc_ref[...] += part

        @pl.when(f == pl.num_programs(1) - 1)
        def _():
            y_ref[...] = acc_ref[...]

    @pl.when((i >= nv_ref[0]) & (f == 0))
    def _():
        y_ref[...] = jnp.zeros(y_ref.shape, y_ref.dtype)


def ffn_grouped(xs, tile_expert, n_valid, w1, w3, w2, idx, tm):
    p, d = xs.shape
    dff = w1.shape[3]
    tf = _tile(dff, 512)
    nf = dff // tf
    fidx = lambda i, f, nv: jnp.where(i < nv[0], f, nf - 1)
    grid_spec = pltpu.PrefetchScalarGridSpec(
        num_scalar_prefetch=2,
        grid=(p // tm, nf),
        in_specs=[
            pl.BlockSpec((tm, d), lambda i, f, te, nv: (jnp.minimum(i, nv[0] - 1), 0)),
            pl.BlockSpec((None, None, d, tf), lambda i, f, te, nv: (idx, te[i], 0, fidx(i, f, nv))),
            pl.BlockSpec((None, None, d, tf), lambda i, f, te, nv: (idx, te[i], 0, fidx(i, f, nv))),
            pl.BlockSpec((None, None, tf, d), lambda i, f, te, nv: (idx, te[i], fidx(i, f, nv), 0)),
        ],
        out_specs=pl.BlockSpec((tm, d), lambda i, f, te, nv: (i, 0)),
        scratch_shapes=[pltpu.VMEM((tm, d), F32)],
    )
    return pl.pallas_call(
        _ffn_grouped_kernel,
        out_shape=jax.ShapeDtypeStruct((p, d), F32),
        grid_spec=grid_spec,
        compiler_params=_params("arbitrary", "arbitrary"),
        name="ffn_grouped",
    )(tile_expert, n_valid, xs, w1, w3, w2)


ROUTE_E1, ROUTE_E2, ROUTE_R1, ROUTE_R2, ROUTE_G1, ROUTE_G2 = range(6)


def _router_kernel(u_ref, rw_ref, rb_ref, info_ref, cnt_ref, carry_ref, *, n_exp):
    @pl.when(pl.program_id(0) == 0)
    def _():
        carry_ref[...] = jnp.zeros(carry_ref.shape, F32)

    logits = jnp.dot(u_ref[...].astype(BF16), rw_ref[...].astype(BF16), preferred_element_type=F32) + rb_ref[...]
    tm = logits.shape[0]
    lane = lax.broadcasted_iota(jnp.int32, logits.shape, 1).astype(F32)
    logits = jnp.where(lane < n_exp, logits, -jnp.inf)
    m1 = jnp.max(logits, axis=-1, keepdims=True)
    i1 = jnp.min(jnp.where(logits == m1, lane, float(LANES)), axis=-1, keepdims=True)
    rest = jnp.where(lane == i1, -jnp.inf, logits)
    m2 = jnp.max(rest, axis=-1, keepdims=True)
    i2 = jnp.min(jnp.where(rest == m2, lane, float(LANES)), axis=-1, keepdims=True)
    e = jnp.exp(m2 - m1)
    g1 = 1.0 / (1.0 + e)
    g2 = e / (1.0 + e)
    oh1 = (lane == i1).astype(F32)
    oh2 = (lane == i2).astype(F32)
    oh = oh1 + oh2
    r = lax.broadcasted_iota(jnp.int32, (tm, tm), 0)
    c = lax.broadcasted_iota(jnp.int32, (tm, tm), 1)
    before = jnp.dot((c < r).astype(BF16), oh.astype(BF16), preferred_element_type=F32) + carry_ref[...]
    r1 = jnp.sum(before * oh1, axis=-1, keepdims=True)
    r2 = jnp.sum(before * oh2, axis=-1, keepdims=True)
    carry_ref[...] += jnp.sum(oh, axis=0, keepdims=True)
    info = jnp.zeros(logits.shape, F32)
    for slot, val in ((ROUTE_E1, i1), (ROUTE_E2, i2), (ROUTE_R1, r1), (ROUTE_R2, r2), (ROUTE_G1, g1),
                      (ROUTE_G2, g2)):
        info = jnp.where(lane == slot, val, info)
    info_ref[...] = info
    cnt_ref[...] = carry_ref[...]


def router(u, rw, rb):
    t, d = u.shape
    n_exp = rw.shape[1]
    tm = _tile(t, 512)
    rw_p = jnp.zeros((d, LANES), rw.dtype).at[:, :n_exp].set(rw)
    rb_p = jnp.zeros((1, LANES), F32).at[0, :n_exp].set(rb.astype(F32))
    return pl.pallas_call(
        functools.partial(_router_kernel, n_exp=n_exp),
        out_shape=(jax.ShapeDtypeStruct((t, LANES), F32), jax.ShapeDtypeStruct((1, LANES), F32)),
        grid=(t // tm,),
        in_specs=[pl.BlockSpec((tm, d), lambda i: (i, 0)), pl.BlockSpec((d, LANES), lambda i: (0, 0)),
                  pl.BlockSpec((1, LANES), lambda i: (0, 0))],
        out_specs=(pl.BlockSpec((tm, LANES), lambda i: (i, 0)), pl.BlockSpec((1, LANES), lambda i: (0, 0))),
        scratch_shapes=[pltpu.VMEM((1, LANES), F32)],
        compiler_params=_params("arbitrary"),
        name="router",
    )(u, rw_p, rb_p)


def _row_copy(src, si, dst, di, sem):
    return pltpu.make_async_copy(src.at[pl.ds(si, 1)], dst.at[pl.ds(di, 1)], sem)


def _dispatch_kernel(dest_ref, u_ref, xs_in_ref, xs_ref, sem, *, td):
    del xs_in_ref

    def start(t, carry):
        for k in range(TOP_K):
            _row_copy(u_ref, t, xs_ref, dest_ref[k, t], sem).start()
        return carry

    def wait(t, carry):
        for k in range(TOP_K):
            _row_copy(u_ref, t, xs_ref, dest_ref[k, t], sem).wait()
        return carry

    lax.fori_loop(0, td, start, 0)
    lax.fori_loop(0, td, wait, 0)


def dispatch(u, dest, xs_zero):
    t, d = u.shape
    td = dest.shape[-1]
    return pl.pallas_call(
        functools.partial(_dispatch_kernel, td=td),
        out_shape=jax.ShapeDtypeStruct(xs_zero.shape, xs_zero.dtype),
        grid=(t // td,),
        in_specs=[pl.BlockSpec((None, TOP_K, td), lambda i: (i, 0, 0), memory_space=pltpu.SMEM),
                  pl.BlockSpec((td, d), lambda i: (i, 0)),
                  pl.BlockSpec(memory_space=pl.ANY)],
        out_specs=pl.BlockSpec(memory_space=pl.ANY),
        scratch_shapes=[pltpu.SemaphoreType.DMA(())],
        input_output_aliases={2: 0},
        compiler_params=_params("arbitrary"),
        name="moe_dispatch",
    )(dest, u, xs_zero)


def _combine_kernel(dest_ref, h_ref, info_ref, ys_ref, nw_ref, hn_ref, u_ref, buf_ref, sem, *, td):
    def start(t, carry):
        for k in range(TOP_K):
            _row_copy(ys_ref, dest_ref[k, t], buf_ref.at[k], t, sem).start()
        return carry

    def wait(t, carry):
        for k in range(TOP_K):
            _row_copy(ys_ref, dest_ref[k, t], buf_ref.at[k], t, sem).wait()
        return carry

    lax.fori_loop(0, td, start, 0)
    lax.fori_loop(0, td, wait, 0)
    info = info_ref[...]
    g1 = info[:, ROUTE_G1:ROUTE_G1 + 1]
    g2 = info[:, ROUTE_G2:ROUTE_G2 + 1]
    hn = h_ref[...] + g1 * buf_ref[0] + g2 * buf_ref[1]
    hn_ref[...] = hn
    u_ref[...] = _rms(hn, nw_ref[...]).astype(u_ref.dtype)


def combine(h, info, dest, ys, norm_w, u_dtype):
    t, d = h.shape
    td = dest.shape[-1]
    rows = pl.BlockSpec((td, d), lambda i: (i, 0))
    return pl.pallas_call(
        functools.partial(_combine_kernel, td=td),
        out_shape=(jax.ShapeDtypeStruct((t, d), F32), jax.ShapeDtypeStruct((t, d), u_dtype)),
        grid=(t // td,),
        in_specs=[pl.BlockSpec((None, TOP_K, td), lambda i: (i, 0, 0), memory_space=pltpu.SMEM),
                  rows, pl.BlockSpec((td, LANES), lambda i: (i, 0)), pl.BlockSpec(memory_space=pl.ANY),
                  pl.BlockSpec((1, d), lambda i: (0, 0))],
        out_specs=(rows, rows),
        scratch_shapes=[pltpu.VMEM((TOP_K, td, d), F32), pltpu.SemaphoreType.DMA(())],
        compiler_params=_params("arbitrary"),
        name="moe_combine",
    )(dest, h, info, ys, norm_w.reshape(1, d))


MOE_ROW_TILE = 512
MOE_TOKEN_TILE = 256


def moe_layer(u, h, rw, rb, w1, w3, w2, idx, norm_w, u_dtype):
    t, d = u.shape
    n_exp = rw.shape[1]
    tm = MOE_ROW_TILE
    td = _tile(t, MOE_TOKEN_TILE)
    info, cnt = router(u, rw, rb)
    counts = cnt[0, :n_exp].astype(jnp.int32)
    padded = ((counts + tm - 1) // tm) * tm
    ends = jnp.cumsum(padded)
    starts = ends - padded
    e = info[:, ROUTE_E1:ROUTE_E2 + 1].astype(jnp.int32)
    rank = info[:, ROUTE_R1:ROUTE_R2 + 1].astype(jnp.int32)
    dest = (starts[e] + rank).T.reshape(TOP_K, t // td, td).transpose(1, 0, 2)
    n_rows = TOP_K * t + n_exp * tm
    n_tiles = n_rows // tm
    n_valid = (ends[-1] // tm).astype(jnp.int32)
    tile_start = jnp.minimum(jnp.arange(n_tiles, dtype=jnp.int32), n_valid - 1) * tm
    tile_expert = jnp.minimum(jnp.sum(tile_start[:, None] >= ends[None, :], axis=1), n_exp - 1).astype(jnp.int32)
    xs = dispatch(u, dest, jnp.zeros((n_rows, d), u.dtype))
    ys = ffn_grouped(xs, tile_expert, n_valid.reshape(1), w1, w3, w2, idx, tm)
    return combine(h, info, dest, ys, norm_w, u_dtype)


def rope_tables(positions):
    half = ROPE_DIM // 2
    inv_freq = ROPE_THETA ** (-jnp.arange(0, ROPE_DIM, 2, dtype=F32) / ROPE_DIM)
    ang = positions.astype(F32).reshape(-1)[:, None] * inv_freq
    cos, sin = jnp.cos(ang), jnp.sin(ang)
    t = ang.shape[0]
    pad = jnp.zeros((t, DA_QK_DIM - ROPE_DIM), F32)
    c = jnp.concatenate([cos, cos, pad + 1.0], axis=1)
    s1 = jnp.concatenate([-sin, jnp.zeros_like(sin), pad], axis=1)
    s2 = jnp.concatenate([jnp.zeros_like(sin), sin, pad], axis=1)
    rep = lambda a: jnp.concatenate([a] * (LANES // DA_QK_DIM), axis=1)
    return rep(c), rep(s1), rep(s2)


def kernel(x, positions, attn_norm_w, w_in, diff_lambda, diff_subln_w, rel_bias, conv_dw_w, conv_dw_b, conv_ln_w,
           conv_ln_b, w_out, ffn_norm_w, ffn_w1, ffn_w3, ffn_w2, moe_router_w, moe_router_b, moe_w1, moe_w3, moe_w2,
           final_norm_w):
    bsz, seq, d = x.shape
    depth = w_in.shape[0]
    t = bsz * seq
    da_width, cb_width = d // 2, d // 4
    cv = d - da_width - cb_width
    da_heads = da_width // DA_V_DIM
    qk_cols = da_heads * 2 * DA_QK_DIM
    dims = dict(qa=qk_cols, ka=qk_cols, da_heads=da_heads, qb_lo=2 * qk_cols + da_width,
                qb_hi=2 * qk_cols + da_width + cb_width, cb_heads=cb_width // CB_HEAD_DIM,
                cv_lo=2 * qk_cols + da_width + 3 * cb_width, cv=cv)
    rope_c, rope_s1, rope_s2 = rope_tables(positions)
    bias = band_bias(rel_bias.reshape((-1,) + rel_bias.shape[2:]), BAND_TQ)
    w_in, w_out, ffn_w1, ffn_w3, ffn_w2, moe_w1, moe_w3, moe_w2 = (
        w.astype(BF16) for w in (w_in, w_out, ffn_w1, ffn_w3, ffn_w2, moe_w1, moe_w3, moe_w2))
    h = x.reshape(t, d)
    u = rmsnorm(h, attn_norm_w[0], BF16)
    for l in range(depth):
        last = l == depth - 1
        moe = l % 2 == 1
        proj = in_projection(u, w_in, l, rope_c, rope_s1, rope_s2, dims)
        lam_init = 0.8 - 0.6 * math.exp(-0.3 * l)
        oa = diff_attention(proj, diff_lambda[l], diff_subln_w[l], lam_init, dims, bsz, seq)
        ob = band_attention(proj, bias, l, dims, bsz, seq)
        oc = conv_module(proj, conv_dw_w[l], conv_dw_b[l], conv_ln_w[l], conv_ln_b[l], dims, bsz, seq)
        h, u = out_projection(h, oa, ob, oc, w_out, l, ffn_norm_w[l], F32 if moe else BF16)
        next_w = final_norm_w if last else attn_norm_w[(l + 1) % depth]
        next_dtype = F32 if last else BF16
        i = l // 2
        if moe:
            h, u = moe_layer(u, h, moe_router_w[i], moe_router_b[i], moe_w1, moe_w3, moe_w2, i, next_w, next_dtype)
        else:
            h, u = ffn_dense(u, h, ffn_w1, ffn_w3, ffn_w2, i, next_w, next_dtype)
    return u.reshape(bsz, seq, d)
```

```python
import functools
import math

import jax
import jax.numpy as jnp
import numpy as np
from jax import lax
from jax.experimental import pallas as pl
from jax.experimental.pallas import tpu as pltpu

F32 = jnp.float32
BF16 = jnp.bfloat16

CHUNK = 64
DA_QK_DIM = 64
DA_V_DIM = 2 * DA_QK_DIM
CB_HEAD_DIM = 128
CB_LEFT_CHUNKS = 8
REL_CLIP = 128
CONV_WIDTH = 31
ROPE_THETA = 500000.0
ROPE_DIM = DA_QK_DIM // 4
TOP_K = 2
NORM_EPS = 1e-6
NEG = -1e30

LANES = 128
SUBLANES = 8
V7X_VMEM_LIMIT_BYTES = 60000 * 1024

CONV_HALO = 32


def _params(*sem):
    return pltpu.CompilerParams(dimension_semantics=sem, vmem_limit_bytes=V7X_VMEM_LIMIT_BYTES)


def _tile(n, target):
    if n <= target:
        return n
    t = target
    while n % t:
        t -= 8
    return t


def _sigmoid(x):
    return 1.0 / (1.0 + jnp.exp(-x))


def _rms(x, w):
    return x * lax.rsqrt(jnp.mean(x * x, axis=-1, keepdims=True) + NORM_EPS) * w


def _rmsnorm_kernel(x_ref, w_ref, o_ref):
    o_ref[...] = _rms(x_ref[...], w_ref[...]).astype(o_ref.dtype)


def rmsnorm(x, w, out_dtype):
    t, d = x.shape
    tm = _tile(t, 1024)
    return pl.pallas_call(
        _rmsnorm_kernel,
        out_shape=jax.ShapeDtypeStruct((t, d), out_dtype),
        grid=(t // tm,),
        in_specs=[pl.BlockSpec((tm, d), lambda i: (i, 0)), pl.BlockSpec((1, d), lambda i: (0, 0))],
        out_specs=pl.BlockSpec((tm, d), lambda i: (i, 0)),
        compiler_params=_params("parallel"),
        name="rmsnorm",
    )(x, w.reshape(1, d))


def _rope_tile(acc, c, s1, s2):
    half = ROPE_DIM // 2
    parts = []
    for g in range(acc.shape[1] // LANES):
        x = acc[:, g * LANES:(g + 1) * LANES]
        parts.append(x * c + pltpu.roll(x, LANES - half, 1) * s1 + pltpu.roll(x, half, 1) * s2)
    return jnp.concatenate(parts, axis=1)


def _inproj_kernel(u_ref, w_ref, c_ref, s1_ref, s2_ref, o_ref, *, tn, qa_tiles, ka_tiles, qb_lo, qb_hi,
                   qa_scale, qb_scale):
    u = u_ref[...]
    for j in range(o_ref.shape[1] // tn):
        cols = slice(j * tn, (j + 1) * tn)
        acc = jnp.dot(u, w_ref[:, cols], preferred_element_type=F32)
        if j < qa_tiles:
            acc = _rope_tile(acc, c_ref[...], s1_ref[...], s2_ref[...]) * qa_scale
        elif j < qa_tiles + ka_tiles:
            acc = _rope_tile(acc, c_ref[...], s1_ref[...], s2_ref[...])
        elif qb_lo <= j < qb_hi:
            acc = acc * qb_scale
        o_ref[:, cols] = acc.astype(o_ref.dtype)


def in_projection(u, w, layer, rope_c, rope_s1, rope_s2, dims):
    t, d = u.shape
    n = w.shape[2]
    tm = _tile(t, 512)
    tn = 512
    assert n % tn == 0 and dims["qa"] % tn == 0 and dims["ka"] % tn == 0 and dims["qb_lo"] % tn == 0 \
        and dims["qb_hi"] % tn == 0
    kern = functools.partial(
        _inproj_kernel, tn=tn, qa_tiles=dims["qa"] // tn, ka_tiles=dims["ka"] // tn, qb_lo=dims["qb_lo"] // tn,
        qb_hi=dims["qb_hi"] // tn, qa_scale=DA_QK_DIM ** -0.5 * math.log2(math.e), qb_scale=CB_HEAD_DIM ** -0.5)
    tab = pl.BlockSpec((tm, LANES), lambda i: (i, 0))
    return pl.pallas_call(
        kern,
        out_shape=jax.ShapeDtypeStruct((t, n), BF16),
        grid=(t // tm,),
        in_specs=[pl.BlockSpec((tm, d), lambda i: (i, 0)),
                  pl.BlockSpec((None, d, n), lambda i: (layer, 0, 0), pipeline_mode=pl.Buffered(1)),
                  tab, tab, tab],
        out_specs=pl.BlockSpec((tm, n), lambda i: (i, 0)),
        compiler_params=_params("parallel"),
        name="in_projection",
    )(u, w, rope_c, rope_s1, rope_s2)


DA_SUM_ROWS = 16


def _diffattn_kernel(q_ref, k_ref, v_ref, lam_ref, sw_ref, o_ref, qc_ref, vt_ref, m_ref, acc_ref, *, tq, lam_init):
    i = pl.program_id(2)

    @pl.when(i == 0)
    def _():
        def transpose_block(c, carry):
            start = pl.multiple_of(c * tq, tq)
            vt_ref[c, 0:DA_V_DIM, :] = v_ref[pl.ds(start, tq), :].astype(F32).T.astype(BF16)
            vt_ref[c, DA_V_DIM:, :] = jnp.ones((DA_SUM_ROWS, tq), BF16)
            return carry

        lax.fori_loop(0, vt_ref.shape[0], transpose_block, 0)

    q = q_ref[...]
    lane = lax.broadcasted_iota(jnp.int32, q.shape, 1)
    zero = jnp.zeros_like(q)
    qc_ref[0:tq, :] = jnp.where(lane < DA_QK_DIM, q, zero)
    qc_ref[tq:2 * tq, :] = jnp.where(lane >= DA_QK_DIM, q, zero)
    m_ref[...] = jnp.full(m_ref.shape, NEG, F32)
    acc_ref[...] = jnp.zeros(acc_ref.shape, F32)

    def scores(j, masked):
        k = k_ref[pl.ds(pl.multiple_of(j * tq, tq), tq), :]
        s = lax.dot_general(k, qc_ref[...], (((1,), (1,)), ((), ())), preferred_element_type=F32)
        if masked:
            kc = lax.broadcasted_iota(jnp.int32, s.shape, 0) // CHUNK
            qi = lax.broadcasted_iota(jnp.int32, s.shape, 1)
            qcx = jnp.where(qi >= tq, qi - tq, qi) // CHUNK
            s = jnp.where(kc <= qcx, s, NEG)
        return s

    def accumulate(j, s):
        m_prev = m_ref[...]
        m_new = jnp.maximum(m_prev, jnp.max(s, axis=0, keepdims=True))
        alpha = jnp.exp2(m_prev - m_new)
        p = jnp.exp2(s - m_new).astype(BF16)
        acc_ref[...] = alpha * acc_ref[...] + jnp.dot(vt_ref[j], p, preferred_element_type=F32)
        m_ref[...] = m_new

    def pair(j0, second_masked):
        s0 = scores(j0, False)
        s1 = scores(j0 + 1, second_masked)
        accumulate(j0, s0)
        accumulate(j0 + 1, s1)

    def body(jj, carry):
        pair(2 * jj, False)
        return carry

    lax.fori_loop(0, i // 2, body, 0)

    @pl.when(i % 2 == 1)
    def _():
        pair(i - 1, True)

    @pl.when(i % 2 == 0)
    def _():
        accumulate(i, scores(i, True))

    lv = lam_ref[...]
    lam = (jnp.exp(jnp.sum(lv[0:1] * lv[1:2], keepdims=True))
           - jnp.exp(jnp.sum(lv[2:3] * lv[3:4], keepdims=True)) + lam_init)
    o_t = acc_ref[0:DA_V_DIM, :] / acc_ref[DA_V_DIM:DA_V_DIM + 1, :]
    o = (o_t[:, 0:tq] - lam * o_t[:, tq:2 * tq]).T
    o_ref[...] = (_rms(o, sw_ref[...]) * (1.0 - lam_init)).astype(o_ref.dtype)


def diff_attention(proj, lam_vec, subln_w, lam_init, dims, bsz, seq):
    heads = dims["da_heads"]
    tq = _tile(seq, 512)
    p3 = proj.reshape(bsz, seq, proj.shape[-1])
    kcol = dims["qa"] // LANES
    vcol = (dims["qa"] + dims["ka"]) // LANES
    kern = functools.partial(_diffattn_kernel, tq=tq, lam_init=lam_init)
    out = pl.pallas_call(
        kern,
        out_shape=jax.ShapeDtypeStruct((bsz, seq, heads * DA_V_DIM), BF16),
        grid=(bsz, heads, seq // tq),
        in_specs=[
            pl.BlockSpec((None, tq, LANES), lambda b, h, i: (b, i, h)),
            pl.BlockSpec((None, seq, LANES), lambda b, h, i: (b, 0, kcol + h)),
            pl.BlockSpec((None, seq, LANES), lambda b, h, i: (b, 0, vcol + h)),
            pl.BlockSpec(lam_vec.shape, lambda b, h, i: (0, 0)),
            pl.BlockSpec((1, DA_V_DIM), lambda b, h, i: (0, 0)),
        ],
        out_specs=pl.BlockSpec((None, tq, DA_V_DIM), lambda b, h, i: (b, i, h)),
        scratch_shapes=[pltpu.VMEM((2 * tq, LANES), BF16),
                        pltpu.VMEM((seq // tq, DA_V_DIM + DA_SUM_ROWS, tq), BF16),
                        pltpu.VMEM((1, 2 * tq), F32), pltpu.VMEM((DA_V_DIM + DA_SUM_ROWS, 2 * tq), F32)],
        compiler_params=_params("parallel", "parallel", "arbitrary"),
        name="diff_attention",
    )(p3, p3, p3, lam_vec, subln_w.reshape(1, DA_V_DIM))
    return out.reshape(bsz * seq, heads * DA_V_DIM)


def _bandattn_kernel(q_ref, kp_ref, kc_ref, vp_ref, vc_ref, bias_ref, o_ref, *, tq):
    i = pl.program_id(2)
    k = jnp.concatenate([kp_ref[...], kc_ref[...]], axis=0)
    v = jnp.concatenate([vp_ref[...], vc_ref[...]], axis=0)
    s = lax.dot_general(q_ref[...], k, (((1,), (1,)), ((), ())), preferred_element_type=F32) + bias_ref[...]
    col = lax.broadcasted_iota(jnp.int32, s.shape, 1)
    s = jnp.where((i == 0) & (col < tq), NEG, s)
    p = jnp.exp(s - jnp.max(s, axis=-1, keepdims=True))
    p = p / jnp.sum(p, axis=-1, keepdims=True)
    o_ref[...] = jnp.dot(p.astype(BF16), v, preferred_element_type=F32).astype(o_ref.dtype)


def band_bias(rel_bias, tq):
    assert tq > REL_CLIP
    n_h = rel_bias.shape[0]
    p = 3 * tq
    rb = rel_bias.astype(F32)
    far = jnp.broadcast_to(rb[:, 2 * REL_CLIP:], (n_h, p))
    behind = jnp.broadcast_to(rb[:, :1], (n_h, p))
    r = jnp.concatenate([far[:, :tq - REL_CLIP + 1], rb[:, 1:2 * REL_CLIP][:, ::-1], behind[:, :tq - REL_CLIP],
                         far[:, :tq]], axis=1)
    toep = jnp.tile(r, (1, tq))[:, :tq * (p - 1)].reshape(n_h, tq, p - 1)[:, :, :2 * tq]
    qc = np.arange(tq)[:, None] // CHUNK
    kc = np.arange(2 * tq)[None, :] // CHUNK
    visible = (kc >= qc) & (kc <= qc + CB_LEFT_CHUNKS)
    return jnp.where(visible[None], toep, NEG)


BAND_TQ = CB_LEFT_CHUNKS * CHUNK


def band_attention(proj, bias, layer, dims, bsz, seq):
    heads = dims["cb_heads"]
    tq = BAND_TQ
    assert seq % tq == 0
    p3 = proj.reshape(bsz, seq, proj.shape[-1])
    qcol = dims["qb_lo"] // LANES
    kcol = qcol + heads
    vcol = kcol + heads
    prev = lambda col: pl.BlockSpec((None, tq, LANES), lambda b, h, i: (b, jnp.maximum(i - 1, 0), col + h))
    cur = lambda col: pl.BlockSpec((None, tq, LANES), lambda b, h, i: (b, i, col + h))
    out = pl.pallas_call(
        functools.partial(_bandattn_kernel, tq=tq),
        out_shape=jax.ShapeDtypeStruct((bsz, seq, heads * CB_HEAD_DIM), BF16),
        grid=(bsz, heads, seq // tq),
        in_specs=[cur(qcol), prev(kcol), cur(kcol), prev(vcol), cur(vcol),
                  pl.BlockSpec((None, tq, 2 * tq), lambda b, h, i: (layer * heads + h, 0, 0))],
        out_specs=pl.BlockSpec((None, tq, CB_HEAD_DIM), lambda b, h, i: (b, i, h)),
        compiler_params=_params("parallel", "parallel", "arbitrary"),
        name="band_attention",
    )(p3, p3, p3, p3, p3, bias)
    return out.reshape(bsz * seq, heads * CB_HEAD_DIM)


def _conv_kernel(a_ref, g_ref, ap_ref, gp_ref, w_ref, b_ref, lnw_ref, lnb_ref, o_ref, u_ref, *, tc, sub):
    i = pl.program_id(1)
    up = ap_ref[...].astype(F32) * _sigmoid(gp_ref[...].astype(F32))
    u_ref[0, 0:CONV_HALO, :] = jnp.where(i > 0, up, 0.0)
    u_ref[0, CONV_HALO:CONV_HALO + tc, :] = a_ref[...].astype(F32) * _sigmoid(g_ref[...].astype(F32))
    n_shift = CONV_HALO + tc - SUBLANES
    for s in range(1, SUBLANES):
        u_ref[s, 0:n_shift, :] = u_ref[0, s:s + n_shift, :]
    first = CONV_HALO - (CONV_WIDTH - 1)

    def body(r, carry):
        r0 = pl.multiple_of(r * sub, sub)
        acc = jnp.zeros((sub, u_ref.shape[2]), F32) + b_ref[...]
        for j in range(CONV_WIDTH):
            off = first + j
            acc = acc + w_ref[j:j + 1, :] * u_ref[off % SUBLANES, pl.ds(r0 + off - off % SUBLANES, sub), :]
        xc = acc - jnp.mean(acc, axis=-1, keepdims=True)
        y = xc * lax.rsqrt(jnp.mean(xc * xc, axis=-1, keepdims=True) + NORM_EPS) * lnw_ref[...] + lnb_ref[...]
        o_ref[pl.ds(r0, sub), :] = (y * _sigmoid(y)).astype(o_ref.dtype)
        return carry

    lax.fori_loop(0, tc // sub, body, 0)


def conv_module(proj, dw_w, dw_b, ln_w, ln_b, dims, bsz, seq):
    c = dims["cv"]
    tc = _tile(seq, 512)
    sub = 32
    p3 = proj.reshape(bsz, seq, proj.shape[-1])
    acol = dims["cv_lo"] // c
    gcol = acol + 1
    per_blk = tc // CONV_HALO
    cur = lambda col: pl.BlockSpec((None, tc, c), lambda b, i: (b, i, col))
    prev = lambda col: pl.BlockSpec((None, CONV_HALO, c), lambda b, i: (b, jnp.maximum(i * per_blk - 1, 0), col))
    row = pl.BlockSpec((1, c), lambda b, i: (0, 0))
    out = pl.pallas_call(
        functools.partial(_conv_kernel, tc=tc, sub=sub),
        out_shape=jax.ShapeDtypeStruct((bsz, seq, c), BF16),
        grid=(bsz, seq // tc),
        in_specs=[cur(acol), cur(gcol), prev(acol), prev(gcol),
                  pl.BlockSpec((CONV_WIDTH, c), lambda b, i: (0, 0)), row, row, row],
        out_specs=pl.BlockSpec((None, tc, c), lambda b, i: (b, i, 0)),
        scratch_shapes=[pltpu.VMEM((SUBLANES, CONV_HALO + tc, c), F32)],
        compiler_params=_params("parallel", "arbitrary"),
        name="conv_module",
    )(p3, p3, p3, p3, dw_w, dw_b.reshape(1, c), ln_w.reshape(1, c), ln_b.reshape(1, c))
    return out.reshape(bsz * seq, c)


def _outproj_kernel(h_ref, oa_ref, ob_ref, oc_ref, w_ref, nw_ref, hn_ref, u_ref, *, ka, kb):
    acc = h_ref[...]
    acc = acc + jnp.dot(oa_ref[...], w_ref[0:ka, :].astype(BF16), preferred_element_type=F32)
    acc = acc + jnp.dot(ob_ref[...], w_ref[ka:ka + kb, :].astype(BF16), preferred_element_type=F32)
    acc = acc + jnp.dot(oc_ref[...], w_ref[ka + kb:, :].astype(BF16), preferred_element_type=F32)
    hn_ref[...] = acc
    u_ref[...] = _rms(acc, nw_ref[...]).astype(u_ref.dtype)


def out_projection(h, oa, ob, oc, w, layer, norm_w, u_dtype):
    t, d = h.shape
    tm = _tile(t, 512)
    ka, kb, kc = oa.shape[1], ob.shape[1], oc.shape[1]
    rows = lambda k: pl.BlockSpec((tm, k), lambda i: (i, 0))
    return pl.pallas_call(
        functools.partial(_outproj_kernel, ka=ka, kb=kb),
        out_shape=(jax.ShapeDtypeStruct((t, d), F32), jax.ShapeDtypeStruct((t, d), u_dtype)),
        grid=(t // tm,),
        in_specs=[rows(d), rows(ka), rows(kb), rows(kc),
                  pl.BlockSpec((None,) + w.shape[1:], lambda i: (layer, 0, 0)),
                  pl.BlockSpec((1, d), lambda i: (0, 0))],
        out_specs=(rows(d), rows(d)),
        compiler_params=_params("parallel"),
        name="out_projection",
    )(h, oa, ob, oc, w, norm_w.reshape(1, d))


def _swiglu_steps(f, nf, x_ref, w1_ref, w3_ref, w2_ref, g_ref, acc_ref, init_acc):
    def gated():
        x = x_ref[...]
        h1 = jnp.dot(x, w1_ref[...], preferred_element_type=F32)
        h3 = jnp.dot(x, w3_ref[...], preferred_element_type=F32)
        return (h1 * _sigmoid(h1) * h3).astype(g_ref.dtype)

    @pl.when(f == 0)
    def _():
        g_ref[...] = gated()
        init_acc()

    @pl.when((f > 0) & (f < nf))
    def _():
        part = jnp.dot(g_ref[...], w2_ref[...], preferred_element_type=F32)
        g_new = gated()
        acc_ref[...] += part
        g_ref[...] = g_new

    @pl.when(f == nf)
    def _():
        acc_ref[...] += jnp.dot(g_ref[...], w2_ref[...], preferred_element_type=F32)


def _ffn_dense_kernel(x_ref, h_ref, w1_ref, w3_ref, w2_ref, nw_ref, hn_ref, u_ref, g_ref, acc_ref):
    f = pl.program_id(1)
    nf = pl.num_programs(1) - 1

    def init_acc():
        acc_ref[...] = h_ref[...]

    _swiglu_steps(f, nf, x_ref, w1_ref, w3_ref, w2_ref, g_ref, acc_ref, init_acc)

    @pl.when(f == nf)
    def _():
        hn = acc_ref[...]
        hn_ref[...] = hn
        u_ref[...] = _rms(hn, nw_ref[...]).astype(u_ref.dtype)


def ffn_dense(u, h, w1, w3, w2, idx, norm_w, u_dtype):
    t, d = u.shape
    dff = w1.shape[2]
    tm = _tile(t, 512)
    tf = _tile(dff, 512)
    nf = dff // tf
    rows = pl.BlockSpec((tm, d), lambda i, f: (i, 0))
    w_up = pl.BlockSpec((None, d, tf), lambda i, f: (idx, 0, jnp.minimum(f, nf - 1)))
    w_down = pl.BlockSpec((None, tf, d), lambda i, f: (idx, jnp.maximum(f - 1, 0), 0))
    return pl.pallas_call(
        _ffn_dense_kernel,
        out_shape=(jax.ShapeDtypeStruct((t, d), F32), jax.ShapeDtypeStruct((t, d), u_dtype)),
        grid=(t // tm, nf + 1),
        in_specs=[rows, rows, w_up, w_up, w_down, pl.BlockSpec((1, d), lambda i, f: (0, 0))],
        out_specs=(rows, rows),
        scratch_shapes=[pltpu.VMEM((tm, tf), BF16), pltpu.VMEM((tm, d), F32)],
        compiler_params=_params("parallel", "arbitrary"),
        name="ffn_dense",
    )(u, h, w1, w3, w2, norm_w.reshape(1, d))


def _ffn_grouped_kernel(te_ref, nv_ref, x_ref, w1_ref, w3_ref, w2_ref, y_ref, xb_ref, g_ref, acc_ref):
    i = pl.program_id(0)
    f = pl.program_id(1)
    nf = pl.num_programs(1) - 1

    @pl.when(i < nv_ref[0])
    def _():
        @pl.when(f == 0)
        def _():
            xb_ref[...] = x_ref[...].astype(xb_ref.dtype)

        def init_acc():
            acc_ref[...] = jnp.zeros(acc_ref.shape, F32)

        _swiglu_steps(f, nf, xb_ref, w1_ref, w3_ref, w2_ref, g_ref, acc_ref, init_acc)

        @pl.when(f == nf)
        def _():
            y_ref[...] = acc_ref[...]

    @pl.when((i >= nv_ref[0]) & (f == 0))
    def _():
        y_ref[...] = jnp.zeros(y_ref.shape, y_ref.dtype)


def ffn_grouped(xs, tile_expert, n_valid, w1, w3, w2, idx, tm):
    p, d = xs.shape
    dff = w1.shape[3]
    tf = _tile(dff, 512)
    nf = dff // tf
    up_idx = lambda i, f, nv: jnp.where(i < nv[0], jnp.minimum(f, nf - 1), nf - 1)
    down_idx = lambda i, f, nv: jnp.where(i < nv[0], jnp.maximum(f - 1, 0), nf - 1)
    w_up = pl.BlockSpec((None, None, d, tf), lambda i, f, te, nv: (idx, te[i], 0, up_idx(i, f, nv)))
    grid_spec = pltpu.PrefetchScalarGridSpec(
        num_scalar_prefetch=2,
        grid=(p // tm, nf + 1),
        in_specs=[
            pl.BlockSpec((tm, d), lambda i, f, te, nv: (jnp.minimum(i, nv[0] - 1), 0)),
            w_up, w_up,
            pl.BlockSpec((None, None, tf, d), lambda i, f, te, nv: (idx, te[i], down_idx(i, f, nv), 0)),
        ],
        out_specs=pl.BlockSpec((tm, d), lambda i, f, te, nv: (i, 0)),
        scratch_shapes=[pltpu.VMEM((tm, d), BF16), pltpu.VMEM((tm, tf), BF16), pltpu.VMEM((tm, d), F32)],
    )
    return pl.pallas_call(
        _ffn_grouped_kernel,
        out_shape=jax.ShapeDtypeStruct((p, d), F32),
        grid_spec=grid_spec,
        compiler_params=_params("arbitrary", "arbitrary"),
        name="ffn_grouped",
    )(tile_expert, n_valid, xs, w1, w3, w2)


ROUTE_E1, ROUTE_E2, ROUTE_R1, ROUTE_R2, ROUTE_G1, ROUTE_G2 = range(6)


def _router_kernel(u_ref, rw_ref, rb_ref, info_ref, cnt_ref, carry_ref, *, n_exp):
    @pl.when(pl.program_id(0) == 0)
    def _():
        carry_ref[...] = jnp.zeros(carry_ref.shape, F32)

    logits = jnp.dot(u_ref[...].astype(BF16), rw_ref[...].astype(BF16), preferred_element_type=F32) + rb_ref[...]
    tm = logits.shape[0]
    lane = lax.broadcasted_iota(jnp.int32, logits.shape, 1).astype(F32)
    logits = jnp.where(lane < n_exp, logits, -jnp.inf)
    m1 = jnp.max(logits, axis=-1, keepdims=True)
    i1 = jnp.min(jnp.where(logits == m1, lane, float(LANES)), axis=-1, keepdims=True)
    rest = jnp.where(lane == i1, -jnp.inf, logits)
    m2 = jnp.max(rest, axis=-1, keepdims=True)
    i2 = jnp.min(jnp.where(rest == m2, lane, float(LANES)), axis=-1, keepdims=True)
    e = jnp.exp(m2 - m1)
    g1 = 1.0 / (1.0 + e)
    g2 = e / (1.0 + e)
    oh1 = (lane == i1).astype(F32)
    oh2 = (lane == i2).astype(F32)
    oh = oh1 + oh2
    r = lax.broadcasted_iota(jnp.int32, (tm, tm), 0)
    c = lax.broadcasted_iota(jnp.int32, (tm, tm), 1)
    before = jnp.dot((c < r).astype(BF16), oh.astype(BF16), preferred_element_type=F32) + carry_ref[...]
    r1 = jnp.sum(before * oh1, axis=-1, keepdims=True)
    r2 = jnp.sum(before * oh2, axis=-1, keepdims=True)
    carry_ref[...] += jnp.sum(oh, axis=0, keepdims=True)
    info = jnp.zeros(logits.shape, F32)
    for slot, val in ((ROUTE_E1, i1), (ROUTE_E2, i2), (ROUTE_R1, r1), (ROUTE_R2, r2), (ROUTE_G1, g1),
                      (ROUTE_G2, g2)):
        info = jnp.where(lane == slot, val, info)
    info_ref[...] = info
    cnt_ref[...] = carry_ref[...]


def router(u, rw, rb):
    t, d = u.shape
    n_exp = rw.shape[1]
    tm = _tile(t, 512)
    rw_p = jnp.zeros((d, LANES), rw.dtype).at[:, :n_exp].set(rw)
    rb_p = jnp.zeros((1, LANES), F32).at[0, :n_exp].set(rb.astype(F32))
    return pl.pallas_call(
        functools.partial(_router_kernel, n_exp=n_exp),
        out_shape=(jax.ShapeDtypeStruct((t, LANES), F32), jax.ShapeDtypeStruct((1, LANES), F32)),
        grid=(t // tm,),
        in_specs=[pl.BlockSpec((tm, d), lambda i: (i, 0)), pl.BlockSpec((d, LANES), lambda i: (0, 0)),
                  pl.BlockSpec((1, LANES), lambda i: (0, 0))],
        out_specs=(pl.BlockSpec((tm, LANES), lambda i: (i, 0)), pl.BlockSpec((1, LANES), lambda i: (0, 0))),
        scratch_shapes=[pltpu.VMEM((1, LANES), F32)],
        compiler_params=_params("arbitrary"),
        name="router",
    )(u, rw_p, rb_p)


def _row_copy(src, si, dst, di, sem):
    return pltpu.make_async_copy(src.at[pl.ds(si, 1)], dst.at[pl.ds(di, 1)], sem)


def _dispatch_kernel(dest_ref, u_ref, xs_in_ref, xs_ref, sem, *, td):
    del xs_in_ref

    def start(t, carry):
        for k in range(TOP_K):
            _row_copy(u_ref, t, xs_ref, dest_ref[k, t], sem).start()
        return carry

    def wait(t, carry):
        for k in range(TOP_K):
            _row_copy(u_ref, t, xs_ref, dest_ref[k, t], sem).wait()
        return carry

    lax.fori_loop(0, td, start, 0)
    lax.fori_loop(0, td, wait, 0)


def dispatch(u, dest, xs_zero):
    t, d = u.shape
    td = dest.shape[-1]
    return pl.pallas_call(
        functools.partial(_dispatch_kernel, td=td),
        out_shape=jax.ShapeDtypeStruct(xs_zero.shape, xs_zero.dtype),
        grid=(t // td,),
        in_specs=[pl.BlockSpec((None, TOP_K, td), lambda i: (i, 0, 0), memory_space=pltpu.SMEM),
                  pl.BlockSpec((td, d), lambda i: (i, 0)),
                  pl.BlockSpec(memory_space=pl.ANY)],
        out_specs=pl.BlockSpec(memory_space=pl.ANY),
        scratch_shapes=[pltpu.SemaphoreType.DMA(())],
        input_output_aliases={2: 0},
        compiler_params=_params("arbitrary"),
        name="moe_dispatch",
    )(dest, u, xs_zero)


def _combine_kernel(dest_ref, h_ref, info_ref, ys_ref, nw_ref, hn_ref, u_ref, buf_ref, sem, *, td):
    def start(t, carry):
        for k in range(TOP_K):
            _row_copy(ys_ref, dest_ref[k, t], buf_ref.at[k], t, sem).start()
        return carry

    def wait(t, carry):
        for k in range(TOP_K):
            _row_copy(ys_ref, dest_ref[k, t], buf_ref.at[k], t, sem).wait()
        return carry

    lax.fori_loop(0, td, start, 0)
    lax.fori_loop(0, td, wait, 0)
    info = info_ref[...]
    g1 = info[:, ROUTE_G1:ROUTE_G1 + 1]
    g2 = info[:, ROUTE_G2:ROUTE_G2 + 1]
    hn = h_ref[...] + g1 * buf_ref[0] + g2 * buf_ref[1]
    hn_ref[...] = hn
    u_ref[...] = _rms(hn, nw_ref[...]).astype(u_ref.dtype)


def combine(h, info, dest, ys, norm_w, u_dtype):
    t, d = h.shape
    td = dest.shape[-1]
    rows = pl.BlockSpec((td, d), lambda i: (i, 0))
    return pl.pallas_call(
        functools.partial(_combine_kernel, td=td),
        out_shape=(jax.ShapeDtypeStruct((t, d), F32), jax.ShapeDtypeStruct((t, d), u_dtype)),
        grid=(t // td,),
        in_specs=[pl.BlockSpec((None, TOP_K, td), lambda i: (i, 0, 0), memory_space=pltpu.SMEM),
                  rows, pl.BlockSpec((td, LANES), lambda i: (i, 0)), pl.BlockSpec(memory_space=pl.ANY),
                  pl.BlockSpec((1, d), lambda i: (0, 0))],
        out_specs=(rows, rows),
        scratch_shapes=[pltpu.VMEM((TOP_K, td, d), F32), pltpu.SemaphoreType.DMA(())],
        compiler_params=_params("arbitrary"),
        name="moe_combine",
    )(dest, h, info, ys, norm_w.reshape(1, d))


MOE_ROW_TILE = 512
MOE_TOKEN_TILE = 256


def moe_layer(u, h, rw, rb, w1, w3, w2, idx, norm_w, u_dtype):
    t, d = u.shape
    n_exp = rw.shape[1]
    tm = MOE_ROW_TILE
    td = _tile(t, MOE_TOKEN_TILE)
    info, cnt = router(u, rw, rb)
    counts = cnt[0, :n_exp].astype(jnp.int32)
    padded = ((counts + tm - 1) // tm) * tm
    ends = jnp.cumsum(padded)
    starts = ends - padded
    e = info[:, ROUTE_E1:ROUTE_E2 + 1].astype(jnp.int32)
    rank = info[:, ROUTE_R1:ROUTE_R2 + 1].astype(jnp.int32)
    dest = (starts[e] + rank).T.reshape(TOP_K, t // td, td).transpose(1, 0, 2)
    n_rows = TOP_K * t + n_exp * tm
    n_tiles = n_rows // tm
    n_valid = (ends[-1] // tm).astype(jnp.int32)
    tile_start = jnp.minimum(jnp.arange(n_tiles, dtype=jnp.int32), n_valid - 1) * tm
    tile_expert = jnp.minimum(jnp.sum(tile_start[:, None] >= ends[None, :], axis=1), n_exp - 1).astype(jnp.int32)
    xs = dispatch(u, dest, jnp.zeros((n_rows, d), u.dtype))
    ys = ffn_grouped(xs, tile_expert, n_valid.reshape(1), w1, w3, w2, idx, tm)
    return combine(h, info, dest, ys, norm_w, u_dtype)


def rope_tables(positions):
    half = ROPE_DIM // 2
    inv_freq = ROPE_THETA ** (-jnp.arange(0, ROPE_DIM, 2, dtype=F32) / ROPE_DIM)
    ang = positions.astype(F32).reshape(-1)[:, None] * inv_freq
    cos, sin = jnp.cos(ang), jnp.sin(ang)
    t = ang.shape[0]
    pad = jnp.zeros((t, DA_QK_DIM - ROPE_DIM), F32)
    c = jnp.concatenate([cos, cos, pad + 1.0], axis=1)
    s1 = jnp.concatenate([-sin, jnp.zeros_like(sin), pad], axis=1)
    s2 = jnp.concatenate([jnp.zeros_like(sin), sin, pad], axis=1)
    rep = lambda a: jnp.concatenate([a] * (LANES // DA_QK_DIM), axis=1)
    return rep(c), rep(s1), rep(s2)


def kernel(x, positions, attn_norm_w, w_in, diff_lambda, diff_subln_w, rel_bias, conv_dw_w, conv_dw_b, conv_ln_w,
           conv_ln_b, w_out, ffn_norm_w, ffn_w1, ffn_w3, ffn_w2, moe_router_w, moe_router_b, moe_w1, moe_w3, moe_w2,
           final_norm_w):
    bsz, seq, d = x.shape
    depth = w_in.shape[0]
    t = bsz * seq
    da_width, cb_width = d // 2, d // 4
    cv = d - da_width - cb_width
    da_heads = da_width // DA_V_DIM
    qk_cols = da_heads * 2 * DA_QK_DIM
    dims = dict(qa=qk_cols, ka=qk_cols, da_heads=da_heads, qb_lo=2 * qk_cols + da_width,
                qb_hi=2 * qk_cols + da_width + cb_width, cb_heads=cb_width // CB_HEAD_DIM,
                cv_lo=2 * qk_cols + da_width + 3 * cb_width, cv=cv)
    rope_c, rope_s1, rope_s2 = rope_tables(positions)
    bias = band_bias(rel_bias.reshape((-1,) + rel_bias.shape[2:]), BAND_TQ)
    w_in, w_out, ffn_w1, ffn_w3, ffn_w2, moe_w1, moe_w3, moe_w2 = (
        w.astype(BF16) for w in (w_in, w_out, ffn_w1, ffn_w3, ffn_w2, moe_w1, moe_w3, moe_w2))
    h = x.reshape(t, d)
    u = rmsnorm(h, attn_norm_w[0], BF16)
    for l in range(depth):
        last = l == depth - 1
        moe = l % 2 == 1
        proj = in_projection(u, w_in, l, rope_c, rope_s1, rope_s2, dims)
        lam_init = 0.8 - 0.6 * math.exp(-0.3 * l)
        oa = diff_attention(proj, diff_lambda[l], diff_subln_w[l], lam_init, dims, bsz, seq)
        ob = band_attention(proj, bias, l, dims, bsz, seq)
        oc = conv_module(proj, conv_dw_w[l], conv_dw_b[l], conv_ln_w[l], conv_ln_b[l], dims, bsz, seq)
        h, u = out_projection(h, oa, ob, oc, w_out, l, ffn_norm_w[l], F32 if moe else BF16)
        next_w = final_norm_w if last else attn_norm_w[(l + 1) % depth]
        next_dtype = F32 if last else BF16
        i = l // 2
        if moe:
            h, u = moe_layer(u, h, moe_router_w[i], moe_router_b[i], moe_w1, moe_w3, moe_w2, i, next_w, next_dtype)
        else:
            h, u = ffn_dense(u, h, ffn_w1, ffn_w3, ffn_w2, i, next_w, next_dtype)
    return u.reshape(bsz, seq, d)
```

```python
import functools
import math

import jax
import jax.numpy as jnp
import numpy as np
from jax import lax
from jax.experimental import pallas as pl
from jax.experimental.pallas import tpu as pltpu

F32 = jnp.float32
BF16 = jnp.bfloat16

CHUNK = 64
DA_QK_DIM = 64
DA_V_DIM = 2 * DA_QK_DIM
CB_HEAD_DIM = 128
CB_LEFT_CHUNKS = 8
REL_CLIP = 128
CONV_WIDTH = 31
ROPE_THETA = 500000.0
ROPE_DIM = DA_QK_DIM // 4
TOP_K = 2
NORM_EPS = 1e-6
NEG = -1e30

LANES = 128
SUBLANES = 8
V7X_VMEM_LIMIT_BYTES = 60000 * 1024

CONV_HALO = 32


def _params(*sem, flags=None):
    return pltpu.CompilerParams(dimension_semantics=sem, vmem_limit_bytes=V7X_VMEM_LIMIT_BYTES, flags=flags)


def _tile(n, target):
    if n <= target:
        return n
    t = target
    while n % t:
        t -= 8
    return t


def _sigmoid(x):
    return 1.0 / (1.0 + jnp.exp(-x))


def _rms(x, w):
    return x * lax.rsqrt(jnp.mean(x * x, axis=-1, keepdims=True) + NORM_EPS) * w


def _rmsnorm_kernel(x_ref, w_ref, o_ref):
    o_ref[...] = _rms(x_ref[...], w_ref[...]).astype(o_ref.dtype)


def rmsnorm(x, w, out_dtype):
    t, d = x.shape
    tm = _tile(t, 1024)
    return pl.pallas_call(
        _rmsnorm_kernel,
        out_shape=jax.ShapeDtypeStruct((t, d), out_dtype),
        grid=(t // tm,),
        in_specs=[pl.BlockSpec((tm, d), lambda i: (i, 0)), pl.BlockSpec((1, d), lambda i: (0, 0))],
        out_specs=pl.BlockSpec((tm, d), lambda i: (i, 0)),
        compiler_params=_params("parallel"),
        name="rmsnorm",
    )(x, w.reshape(1, d))


def _rope_tile(acc, c, s1, s2):
    half = ROPE_DIM // 2
    parts = []
    for g in range(acc.shape[1] // LANES):
        x = acc[:, g * LANES:(g + 1) * LANES]
        parts.append(x * c + pltpu.roll(x, LANES - half, 1) * s1 + pltpu.roll(x, half, 1) * s2)
    return jnp.concatenate(parts, axis=1)


def _inproj_kernel(u_ref, w_ref, c_ref, s1_ref, s2_ref, o_ref, *, tn, qa_tiles, ka_tiles, qb_lo, qb_hi,
                   qa_scale, qb_scale):
    u = u_ref[...]
    for j in range(o_ref.shape[1] // tn):
        cols = slice(j * tn, (j + 1) * tn)
        acc = jnp.dot(u, w_ref[:, cols], preferred_element_type=F32)
        if j < qa_tiles:
            acc = _rope_tile(acc, c_ref[...], s1_ref[...], s2_ref[...]) * qa_scale
        elif j < qa_tiles + ka_tiles:
            acc = _rope_tile(acc, c_ref[...], s1_ref[...], s2_ref[...])
        elif qb_lo <= j < qb_hi:
            acc = acc * qb_scale
        o_ref[:, cols] = acc.astype(o_ref.dtype)


def in_projection(u, w, layer, rope_c, rope_s1, rope_s2, dims):
    t, d = u.shape
    n = w.shape[2]
    tm = _tile(t, 512)
    tn = 512
    assert n % tn == 0 and dims["qa"] % tn == 0 and dims["ka"] % tn == 0 and dims["qb_lo"] % tn == 0 \
        and dims["qb_hi"] % tn == 0
    kern = functools.partial(
        _inproj_kernel, tn=tn, qa_tiles=dims["qa"] // tn, ka_tiles=dims["ka"] // tn, qb_lo=dims["qb_lo"] // tn,
        qb_hi=dims["qb_hi"] // tn, qa_scale=DA_QK_DIM ** -0.5 * math.log2(math.e), qb_scale=CB_HEAD_DIM ** -0.5)
    tab = pl.BlockSpec((tm, LANES), lambda i: (i, 0))
    return pl.pallas_call(
        kern,
        out_shape=jax.ShapeDtypeStruct((t, n), BF16),
        grid=(t // tm,),
        in_specs=[pl.BlockSpec((tm, d), lambda i: (i, 0)),
                  pl.BlockSpec((None, d, n), lambda i: (layer, 0, 0), pipeline_mode=pl.Buffered(1)),
                  tab, tab, tab],
        out_specs=pl.BlockSpec((tm, n), lambda i: (i, 0)),
        compiler_params=_params("parallel"),
        name="in_projection",
    )(u, w, rope_c, rope_s1, rope_s2)


DA_SUM_ROWS = 16


def _diffattn_kernel(q_ref, k_ref, v_ref, lam_ref, sw_ref, o_ref, qc_ref, vt_ref, s_ref, m_ref, acc_ref, *, tq,
                     lam_init):
    i = pl.program_id(2)

    @pl.when(i == 0)
    def _():
        def transpose_block(c, carry):
            start = pl.multiple_of(c * tq, tq)
            vt_ref[c, 0:DA_V_DIM, :] = v_ref[pl.ds(start, tq), :].astype(F32).T.astype(BF16)
            vt_ref[c, DA_V_DIM:, :] = jnp.ones((DA_SUM_ROWS, tq), BF16)
            return carry

        lax.fori_loop(0, vt_ref.shape[0], transpose_block, 0)

    q = q_ref[...]
    lane = lax.broadcasted_iota(jnp.int32, q.shape, 1)
    zero = jnp.zeros_like(q)
    qc_ref[0:tq, :] = jnp.where(lane < DA_QK_DIM, q, zero)
    qc_ref[tq:2 * tq, :] = jnp.where(lane >= DA_QK_DIM, q, zero)
    m_ref[...] = jnp.full(m_ref.shape, NEG, F32)
    acc_ref[...] = jnp.zeros(acc_ref.shape, F32)

    def scores(j):
        k = k_ref[pl.ds(pl.multiple_of(j * tq, tq), tq), :]
        return lax.dot_general(k, qc_ref[...], (((1,), (1,)), ((), ())), preferred_element_type=F32)

    def accumulate(j, slot, masked):
        s = s_ref[slot]
        if masked:
            kc = lax.broadcasted_iota(jnp.int32, s.shape, 0) // CHUNK
            qi = lax.broadcasted_iota(jnp.int32, s.shape, 1)
            qcx = jnp.where(qi >= tq, qi - tq, qi) // CHUNK
            s = jnp.where(kc <= qcx, s, NEG)
        m_prev = m_ref[...]
        m_new = jnp.maximum(m_prev, jnp.max(s, axis=0, keepdims=True))
        alpha = jnp.exp2(m_prev - m_new)
        p = jnp.exp2(s - m_new).astype(BF16)
        acc_ref[...] = alpha * acc_ref[...] + jnp.dot(vt_ref[j], p, preferred_element_type=F32)
        m_ref[...] = m_new

    s_ref[0] = scores(0)

    def body(jj, carry):
        j0 = 2 * jj
        s_ref[1] = scores(j0 + 1)
        accumulate(j0, 0, False)
        s_ref[0] = scores(j0 + 2)
        accumulate(j0 + 1, 1, False)
        return carry

    lax.fori_loop(0, i // 2, body, 0)

    @pl.when(i % 2 == 1)
    def _():
        s_ref[1] = scores(i)
        accumulate(i - 1, 0, False)
        accumulate(i, 1, True)

    @pl.when(i % 2 == 0)
    def _():
        accumulate(i, 0, True)

    lv = lam_ref[...]
    lam = (jnp.exp(jnp.sum(lv[0:1] * lv[1:2], keepdims=True))
           - jnp.exp(jnp.sum(lv[2:3] * lv[3:4], keepdims=True)) + lam_init)
    o_t = acc_ref[0:DA_V_DIM, :] / acc_ref[DA_V_DIM:DA_V_DIM + 1, :]
    o = (o_t[:, 0:tq] - lam * o_t[:, tq:2 * tq]).T
    o_ref[...] = (_rms(o, sw_ref[...]) * (1.0 - lam_init)).astype(o_ref.dtype)


def diff_attention(proj, lam_vec, subln_w, lam_init, dims, bsz, seq):
    heads = dims["da_heads"]
    tq = _tile(seq, 512)
    p3 = proj.reshape(bsz, seq, proj.shape[-1])
    kcol = dims["qa"] // LANES
    vcol = (dims["qa"] + dims["ka"]) // LANES
    kern = functools.partial(_diffattn_kernel, tq=tq, lam_init=lam_init)
    out = pl.pallas_call(
        kern,
        out_shape=jax.ShapeDtypeStruct((bsz, seq, heads * DA_V_DIM), BF16),
        grid=(bsz, heads, seq // tq),
        in_specs=[
            pl.BlockSpec((None, tq, LANES), lambda b, h, i: (b, i, h)),
            pl.BlockSpec((None, seq, LANES), lambda b, h, i: (b, 0, kcol + h)),
            pl.BlockSpec((None, seq, LANES), lambda b, h, i: (b, 0, vcol + h)),
            pl.BlockSpec(lam_vec.shape, lambda b, h, i: (0, 0)),
            pl.BlockSpec((1, DA_V_DIM), lambda b, h, i: (0, 0)),
        ],
        out_specs=pl.BlockSpec((None, tq, DA_V_DIM), lambda b, h, i: (b, i, h)),
        scratch_shapes=[pltpu.VMEM((2 * tq, LANES), BF16),
                        pltpu.VMEM((seq // tq, DA_V_DIM + DA_SUM_ROWS, tq), BF16),
                        pltpu.VMEM((2, tq, 2 * tq), F32),
                        pltpu.VMEM((1, 2 * tq), F32), pltpu.VMEM((DA_V_DIM + DA_SUM_ROWS, 2 * tq), F32)],
        compiler_params=_params("parallel", "parallel", "arbitrary"),
        name="diff_attention",
    )(p3, p3, p3, lam_vec, subln_w.reshape(1, DA_V_DIM))
    return out.reshape(bsz * seq, heads * DA_V_DIM)


def _bandattn_kernel(q_ref, kp_ref, kc_ref, vp_ref, vc_ref, bias_ref, o_ref, *, tq):
    i = pl.program_id(2)
    k = jnp.concatenate([kp_ref[...], kc_ref[...]], axis=0)
    v = jnp.concatenate([vp_ref[...], vc_ref[...]], axis=0)
    s = lax.dot_general(q_ref[...], k, (((1,), (1,)), ((), ())), preferred_element_type=F32) + bias_ref[...]
    col = lax.broadcasted_iota(jnp.int32, s.shape, 1)
    s = jnp.where((i == 0) & (col < tq), NEG, s)
    p = jnp.exp(s - jnp.max(s, axis=-1, keepdims=True))
    p = p / jnp.sum(p, axis=-1, keepdims=True)
    o_ref[...] = jnp.dot(p.astype(BF16), v, preferred_element_type=F32).astype(o_ref.dtype)


def band_bias(rel_bias, tq):
    assert tq > REL_CLIP
    n_h = rel_bias.shape[0]
    p = 3 * tq
    rb = rel_bias.astype(F32)
    far = jnp.broadcast_to(rb[:, 2 * REL_CLIP:], (n_h, p))
    behind = jnp.broadcast_to(rb[:, :1], (n_h, p))
    r = jnp.concatenate([far[:, :tq - REL_CLIP + 1], rb[:, 1:2 * REL_CLIP][:, ::-1], behind[:, :tq - REL_CLIP],
                         far[:, :tq]], axis=1)
    toep = jnp.tile(r, (1, tq))[:, :tq * (p - 1)].reshape(n_h, tq, p - 1)[:, :, :2 * tq]
    qc = np.arange(tq)[:, None] // CHUNK
    kc = np.arange(2 * tq)[None, :] // CHUNK
    visible = (kc >= qc) & (kc <= qc + CB_LEFT_CHUNKS)
    return jnp.where(visible[None], toep, NEG)


BAND_TQ = CB_LEFT_CHUNKS * CHUNK


def band_attention(proj, bias, layer, dims, bsz, seq):
    heads = dims["cb_heads"]
    tq = BAND_TQ
    assert seq % tq == 0
    p3 = proj.reshape(bsz, seq, proj.shape[-1])
    qcol = dims["qb_lo"] // LANES
    kcol = qcol + heads
    vcol = kcol + heads
    prev = lambda col: pl.BlockSpec((None, tq, LANES), lambda b, h, i: (b, jnp.maximum(i - 1, 0), col + h))
    cur = lambda col: pl.BlockSpec((None, tq, LANES), lambda b, h, i: (b, i, col + h))
    out = pl.pallas_call(
        functools.partial(_bandattn_kernel, tq=tq),
        out_shape=jax.ShapeDtypeStruct((bsz, seq, heads * CB_HEAD_DIM), BF16),
        grid=(bsz, heads, seq // tq),
        in_specs=[cur(qcol), prev(kcol), cur(kcol), prev(vcol), cur(vcol),
                  pl.BlockSpec((None, tq, 2 * tq), lambda b, h, i: (layer * heads + h, 0, 0))],
        out_specs=pl.BlockSpec((None, tq, CB_HEAD_DIM), lambda b, h, i: (b, i, h)),
        compiler_params=_params("parallel", "parallel", "arbitrary"),
        name="band_attention",
    )(p3, p3, p3, p3, p3, bias)
    return out.reshape(bsz * seq, heads * CB_HEAD_DIM)


def _conv_kernel(a_ref, g_ref, ap_ref, gp_ref, w_ref, b_ref, lnw_ref, lnb_ref, o_ref, u_ref, *, tc, sub):
    i = pl.program_id(1)
    up = ap_ref[...].astype(F32) * _sigmoid(gp_ref[...].astype(F32))
    u_ref[0, 0:CONV_HALO, :] = jnp.where(i > 0, up, 0.0)
    u_ref[0, CONV_HALO:CONV_HALO + tc, :] = a_ref[...].astype(F32) * _sigmoid(g_ref[...].astype(F32))
    n_shift = CONV_HALO + tc - SUBLANES
    for s in range(1, SUBLANES):
        u_ref[s, 0:n_shift, :] = u_ref[0, s:s + n_shift, :]
    first = CONV_HALO - (CONV_WIDTH - 1)

    def body(r, carry):
        r0 = pl.multiple_of(r * sub, sub)
        acc = jnp.zeros((sub, u_ref.shape[2]), F32) + b_ref[...]
        for j in range(CONV_WIDTH):
            off = first + j
            acc = acc + w_ref[j:j + 1, :] * u_ref[off % SUBLANES, pl.ds(r0 + off - off % SUBLANES, sub), :]
        xc = acc - jnp.mean(acc, axis=-1, keepdims=True)
        y = xc * lax.rsqrt(jnp.mean(xc * xc, axis=-1, keepdims=True) + NORM_EPS) * lnw_ref[...] + lnb_ref[...]
        o_ref[pl.ds(r0, sub), :] = (y * _sigmoid(y)).astype(o_ref.dtype)
        return carry

    lax.fori_loop(0, tc // sub, body, 0)


def conv_module(proj, dw_w, dw_b, ln_w, ln_b, dims, bsz, seq):
    c = dims["cv"]
    tc = _tile(seq, 512)
    sub = 32
    p3 = proj.reshape(bsz, seq, proj.shape[-1])
    acol = dims["cv_lo"] // c
    gcol = acol + 1
    per_blk = tc // CONV_HALO
    cur = lambda col: pl.BlockSpec((None, tc, c), lambda b, i: (b, i, col))
    prev = lambda col: pl.BlockSpec((None, CONV_HALO, c), lambda b, i: (b, jnp.maximum(i * per_blk - 1, 0), col))
    row = pl.BlockSpec((1, c), lambda b, i: (0, 0))
    out = pl.pallas_call(
        functools.partial(_conv_kernel, tc=tc, sub=sub),
        out_shape=jax.ShapeDtypeStruct((bsz, seq, c), BF16),
        grid=(bsz, seq // tc),
        in_specs=[cur(acol), cur(gcol), prev(acol), prev(gcol),
                  pl.BlockSpec((CONV_WIDTH, c), lambda b, i: (0, 0)), row, row, row],
        out_specs=pl.BlockSpec((None, tc, c), lambda b, i: (b, i, 0)),
        scratch_shapes=[pltpu.VMEM((SUBLANES, CONV_HALO + tc, c), F32)],
        compiler_params=_params("parallel", "arbitrary"),
        name="conv_module",
    )(p3, p3, p3, p3, dw_w, dw_b.reshape(1, c), ln_w.reshape(1, c), ln_b.reshape(1, c))
    return out.reshape(bsz * seq, c)


def _outproj_kernel(h_ref, oa_ref, ob_ref, oc_ref, w_ref, nw_ref, hn_ref, u_ref, *, ka, kb):
    acc = h_ref[...]
    acc = acc + jnp.dot(oa_ref[...], w_ref[0:ka, :].astype(BF16), preferred_element_type=F32)
    acc = acc + jnp.dot(ob_ref[...], w_ref[ka:ka + kb, :].astype(BF16), preferred_element_type=F32)
    acc = acc + jnp.dot(oc_ref[...], w_ref[ka + kb:, :].astype(BF16), preferred_element_type=F32)
    hn_ref[...] = acc
    u_ref[...] = _rms(acc, nw_ref[...]).astype(u_ref.dtype)


def out_projection(h, oa, ob, oc, w, layer, norm_w, u_dtype):
    t, d = h.shape
    tm = _tile(t, 512)
    ka, kb, kc = oa.shape[1], ob.shape[1], oc.shape[1]
    rows = lambda k: pl.BlockSpec((tm, k), lambda i: (i, 0))
    return pl.pallas_call(
        functools.partial(_outproj_kernel, ka=ka, kb=kb),
        out_shape=(jax.ShapeDtypeStruct((t, d), F32), jax.ShapeDtypeStruct((t, d), u_dtype)),
        grid=(t // tm,),
        in_specs=[rows(d), rows(ka), rows(kb), rows(kc),
                  pl.BlockSpec((None,) + w.shape[1:], lambda i: (layer, 0, 0)),
                  pl.BlockSpec((1, d), lambda i: (0, 0))],
        out_specs=(rows(d), rows(d)),
        compiler_params=_params("parallel"),
        name="out_projection",
    )(h, oa, ob, oc, w, norm_w.reshape(1, d))


FFN_HIDDEN_TILE = 512


def tile_up_weights(w):
    *lead, d, dff = w.shape
    tf = _tile(dff, FFN_HIDDEN_TILE)
    return jnp.swapaxes(w.astype(BF16).reshape(*lead, d, dff // tf, tf), -3, -2)


def _swiglu_steps(f, nf, x_ref, w1_ref, w3_ref, w2_ref, g_ref, acc_ref, init_acc):
    def gated():
        x = x_ref[...]
        h1 = jnp.dot(x, w1_ref[...], preferred_element_type=F32)
        h3 = jnp.dot(x, w3_ref[...], preferred_element_type=F32)
        return (h1 * _sigmoid(h1) * h3).astype(g_ref.dtype)

    @pl.when(f == 0)
    def _():
        g_ref[...] = gated()
        init_acc()

    @pl.when((f > 0) & (f < nf))
    def _():
        part = jnp.dot(g_ref[...], w2_ref[...], preferred_element_type=F32)
        g_new = gated()
        acc_ref[...] += part
        g_ref[...] = g_new

    @pl.when(f == nf)
    def _():
        acc_ref[...] += jnp.dot(g_ref[...], w2_ref[...], preferred_element_type=F32)


def _ffn_dense_kernel(x_ref, h_ref, w1_ref, w3_ref, w2_ref, nw_ref, hn_ref, u_ref, g_ref, acc_ref):
    f = pl.program_id(1)
    nf = pl.num_programs(1) - 1

    def init_acc():
        acc_ref[...] = h_ref[...]

    _swiglu_steps(f, nf, x_ref, w1_ref, w3_ref, w2_ref, g_ref, acc_ref, init_acc)

    @pl.when(f == nf)
    def _():
        hn = acc_ref[...]
        hn_ref[...] = hn
        u_ref[...] = _rms(hn, nw_ref[...]).astype(u_ref.dtype)


def ffn_dense(u, h, w1, w3, w2, idx, norm_w, u_dtype):
    t, d = u.shape
    nf, tf = w1.shape[1], w1.shape[3]
    tm = _tile(t, 512)
    rows = pl.BlockSpec((tm, d), lambda i, f: (i, 0))
    w_up = pl.BlockSpec((None, None, d, tf), lambda i, f: (idx, jnp.minimum(f, nf - 1), 0, 0))
    w_down = pl.BlockSpec((None, tf, d), lambda i, f: (idx, jnp.maximum(f - 1, 0), 0))
    return pl.pallas_call(
        _ffn_dense_kernel,
        out_shape=(jax.ShapeDtypeStruct((t, d), F32), jax.ShapeDtypeStruct((t, d), u_dtype)),
        grid=(t // tm, nf + 1),
        in_specs=[rows, rows, w_up, w_up, w_down, pl.BlockSpec((1, d), lambda i, f: (0, 0))],
        out_specs=(rows, rows),
        scratch_shapes=[pltpu.VMEM((tm, tf), BF16), pltpu.VMEM((tm, d), F32)],
        compiler_params=_params("parallel", "arbitrary"),
        name="ffn_dense",
    )(u, h, w1, w3, w2, norm_w.reshape(1, d))


def _ffn_grouped_kernel(te_ref, nv_ref, x_ref, w1_ref, w3_ref, w2_ref, y_ref, xb_ref, g_ref, acc_ref):
    i = pl.program_id(0)
    f = pl.program_id(1)
    nf = pl.num_programs(1) - 1

    @pl.when(i < nv_ref[0])
    def _():
        @pl.when(f == 0)
        def _():
            xb_ref[...] = x_ref[...].astype(xb_ref.dtype)

        def init_acc():
            acc_ref[...] = jnp.zeros(acc_ref.shape, F32)

        _swiglu_steps(f, nf, xb_ref, w1_ref, w3_ref, w2_ref, g_ref, acc_ref, init_acc)

        @pl.when(f == nf)
        def _():
            y_ref[...] = acc_ref[...]

    @pl.when((i >= nv_ref[0]) & (f == 0))
    def _():
        y_ref[...] = jnp.zeros(y_ref.shape, y_ref.dtype)


def ffn_grouped(xs, tile_expert, n_valid, w1, w3, w2, idx, tm):
    p, d = xs.shape
    nf, tf = w1.shape[2], w1.shape[4]
    up_idx = lambda i, f, nv: jnp.where(i < nv[0], jnp.minimum(f, nf - 1), nf - 1)
    down_idx = lambda i, f, nv: jnp.where(i < nv[0], jnp.maximum(f - 1, 0), nf - 1)
    w_up = pl.BlockSpec((None, None, None, d, tf), lambda i, f, te, nv: (idx, te[i], up_idx(i, f, nv), 0, 0))
    grid_spec = pltpu.PrefetchScalarGridSpec(
        num_scalar_prefetch=2,
        grid=(p // tm, nf + 1),
        in_specs=[
            pl.BlockSpec((tm, d), lambda i, f, te, nv: (jnp.minimum(i, nv[0] - 1), 0)),
            w_up, w_up,
            pl.BlockSpec((None, None, tf, d), lambda i, f, te, nv: (idx, te[i], down_idx(i, f, nv), 0)),
        ],
        out_specs=pl.BlockSpec((tm, d), lambda i, f, te, nv: (i, 0)),
        scratch_shapes=[pltpu.VMEM((tm, d), BF16), pltpu.VMEM((tm, tf), BF16), pltpu.VMEM((tm, d), F32)],
    )
    return pl.pallas_call(
        _ffn_grouped_kernel,
        out_shape=jax.ShapeDtypeStruct((p, d), F32),
        grid_spec=grid_spec,
        compiler_params=_params("arbitrary", "arbitrary"),
        name="ffn_grouped",
    )(tile_expert, n_valid, xs, w1, w3, w2)


ROUTE_E1, ROUTE_E2, ROUTE_R1, ROUTE_R2, ROUTE_G1, ROUTE_G2 = range(6)


def _router_kernel(u_ref, rw_ref, rb_ref, info_ref, cnt_ref, carry_ref, *, n_exp):
    @pl.when(pl.program_id(0) == 0)
    def _():
        carry_ref[...] = jnp.zeros(carry_ref.shape, F32)

    logits = jnp.dot(u_ref[...].astype(BF16), rw_ref[...].astype(BF16), preferred_element_type=F32) + rb_ref[...]
    tm = logits.shape[0]
    lane = lax.broadcasted_iota(jnp.int32, logits.shape, 1).astype(F32)
    logits = jnp.where(lane < n_exp, logits, -jnp.inf)
    m1 = jnp.max(logits, axis=-1, keepdims=True)
    i1 = jnp.min(jnp.where(logits == m1, lane, float(LANES)), axis=-1, keepdims=True)
    rest = jnp.where(lane == i1, -jnp.inf, logits)
    m2 = jnp.max(rest, axis=-1, keepdims=True)
    i2 = jnp.min(jnp.where(rest == m2, lane, float(LANES)), axis=-1, keepdims=True)
    e = jnp.exp(m2 - m1)
    g1 = 1.0 / (1.0 + e)
    g2 = e / (1.0 + e)
    oh1 = (lane == i1).astype(F32)
    oh2 = (lane == i2).astype(F32)
    oh = oh1 + oh2
    r = lax.broadcasted_iota(jnp.int32, (tm, tm), 0)
    c = lax.broadcasted_iota(jnp.int32, (tm, tm), 1)
    before = jnp.dot((c < r).astype(BF16), oh.astype(BF16), preferred_element_type=F32) + carry_ref[...]
    r1 = jnp.sum(before * oh1, axis=-1, keepdims=True)
    r2 = jnp.sum(before * oh2, axis=-1, keepdims=True)
    carry_ref[...] += jnp.sum(oh, axis=0, keepdims=True)
    info = jnp.zeros(logits.shape, F32)
    for slot, val in ((ROUTE_E1, i1), (ROUTE_E2, i2), (ROUTE_R1, r1), (ROUTE_R2, r2), (ROUTE_G1, g1),
                      (ROUTE_G2, g2)):
        info = jnp.where(lane == slot, val, info)
    info_ref[...] = info
    cnt_ref[...] = carry_ref[...]


def router(u, rw, rb):
    t, d = u.shape
    n_exp = rw.shape[1]
    tm = _tile(t, 512)
    rw_p = jnp.zeros((d, LANES), rw.dtype).at[:, :n_exp].set(rw)
    rb_p = jnp.zeros((1, LANES), F32).at[0, :n_exp].set(rb.astype(F32))
    return pl.pallas_call(
        functools.partial(_router_kernel, n_exp=n_exp),
        out_shape=(jax.ShapeDtypeStruct((t, LANES), F32), jax.ShapeDtypeStruct((1, LANES), F32)),
        grid=(t // tm,),
        in_specs=[pl.BlockSpec((tm, d), lambda i: (i, 0)), pl.BlockSpec((d, LANES), lambda i: (0, 0)),
                  pl.BlockSpec((1, LANES), lambda i: (0, 0))],
        out_specs=(pl.BlockSpec((tm, LANES), lambda i: (i, 0)), pl.BlockSpec((1, LANES), lambda i: (0, 0))),
        scratch_shapes=[pltpu.VMEM((1, LANES), F32)],
        compiler_params=_params("arbitrary"),
        name="router",
    )(u, rw_p, rb_p)


def _row_copy(src, si, dst, di, sem):
    return pltpu.make_async_copy(src.at[pl.ds(si, 1)], dst.at[pl.ds(di, 1)], sem)


def _dispatch_kernel(dest_ref, u_ref, xs_in_ref, xs_ref, sem, *, td):
    del xs_in_ref

    def start(t, carry):
        for k in range(TOP_K):
            _row_copy(u_ref, t, xs_ref, dest_ref[k, t], sem).start()
        return carry

    def wait(t, carry):
        for k in range(TOP_K):
            _row_copy(u_ref, t, xs_ref, dest_ref[k, t], sem).wait()
        return carry

    lax.fori_loop(0, td, start, 0)
    lax.fori_loop(0, td, wait, 0)


def dispatch(u, dest, xs_zero):
    t, d = u.shape
    td = dest.shape[-1]
    return pl.pallas_call(
        functools.partial(_dispatch_kernel, td=td),
        out_shape=jax.ShapeDtypeStruct(xs_zero.shape, xs_zero.dtype),
        grid=(t // td,),
        in_specs=[pl.BlockSpec((None, TOP_K, td), lambda i: (i, 0, 0), memory_space=pltpu.SMEM),
                  pl.BlockSpec((td, d), lambda i: (i, 0)),
                  pl.BlockSpec(memory_space=pl.ANY)],
        out_specs=pl.BlockSpec(memory_space=pl.ANY),
        scratch_shapes=[pltpu.SemaphoreType.DMA(())],
        input_output_aliases={2: 0},
        compiler_params=_params("arbitrary"),
        name="moe_dispatch",
    )(dest, u, xs_zero)


def _combine_kernel(dest_ref, h_ref, info_ref, ys_ref, nw_ref, hn_ref, u_ref, buf_ref, sem, *, td):
    def start(t, carry):
        for k in range(TOP_K):
            _row_copy(ys_ref, dest_ref[k, t], buf_ref.at[k], t, sem).start()
        return carry

    def wait(t, carry):
        for k in range(TOP_K):
            _row_copy(ys_ref, dest_ref[k, t], buf_ref.at[k], t, sem).wait()
        return carry

    lax.fori_loop(0, td, start, 0)
    lax.fori_loop(0, td, wait, 0)
    info = info_ref[...]
    g1 = info[:, ROUTE_G1:ROUTE_G1 + 1]
    g2 = info[:, ROUTE_G2:ROUTE_G2 + 1]
    hn = h_ref[...] + g1 * buf_ref[0] + g2 * buf_ref[1]
    hn_ref[...] = hn
    u_ref[...] = _rms(hn, nw_ref[...]).astype(u_ref.dtype)


def combine(h, info, dest, ys, norm_w, u_dtype):
    t, d = h.shape
    td = dest.shape[-1]
    rows = pl.BlockSpec((td, d), lambda i: (i, 0))
    return pl.pallas_call(
        functools.partial(_combine_kernel, td=td),
        out_shape=(jax.ShapeDtypeStruct((t, d), F32), jax.ShapeDtypeStruct((t, d), u_dtype)),
        grid=(t // td,),
        in_specs=[pl.BlockSpec((None, TOP_K, td), lambda i: (i, 0, 0), memory_space=pltpu.SMEM),
                  rows, pl.BlockSpec((td, LANES), lambda i: (i, 0)), pl.BlockSpec(memory_space=pl.ANY),
                  pl.BlockSpec((1, d), lambda i: (0, 0))],
        out_specs=(rows, rows),
        scratch_shapes=[pltpu.VMEM((TOP_K, td, d), F32), pltpu.SemaphoreType.DMA(())],
        compiler_params=_params("arbitrary"),
        name="moe_combine",
    )(dest, h, info, ys, norm_w.reshape(1, d))


MOE_ROW_TILE = 512
MOE_TOKEN_TILE = 256


def moe_layer(u, h, rw, rb, w1, w3, w2, idx, norm_w, u_dtype):
    t, d = u.shape
    n_exp = rw.shape[1]
    tm = MOE_ROW_TILE
    td = _tile(t, MOE_TOKEN_TILE)
    info, cnt = router(u, rw, rb)
    counts = cnt[0, :n_exp].astype(jnp.int32)
    padded = ((counts + tm - 1) // tm) * tm
    ends = jnp.cumsum(padded)
    starts = ends - padded
    e = info[:, ROUTE_E1:ROUTE_E2 + 1].astype(jnp.int32)
    rank = info[:, ROUTE_R1:ROUTE_R2 + 1].astype(jnp.int32)
    dest = (starts[e] + rank).T.reshape(TOP_K, t // td, td).transpose(1, 0, 2)
    n_rows = TOP_K * t + n_exp * tm
    n_tiles = n_rows // tm
    n_valid = (ends[-1] // tm).astype(jnp.int32)
    tile_start = jnp.minimum(jnp.arange(n_tiles, dtype=jnp.int32), n_valid - 1) * tm
    tile_expert = jnp.minimum(jnp.sum(tile_start[:, None] >= ends[None, :], axis=1), n_exp - 1).astype(jnp.int32)
    xs = dispatch(u, dest, jnp.zeros((n_rows, d), u.dtype))
    ys = ffn_grouped(xs, tile_expert, n_valid.reshape(1), w1, w3, w2, idx, tm)
    return combine(h, info, dest, ys, norm_w, u_dtype)


def rope_tables(positions):
    half = ROPE_DIM // 2
    inv_freq = ROPE_THETA ** (-jnp.arange(0, ROPE_DIM, 2, dtype=F32) / ROPE_DIM)
    ang = positions.astype(F32).reshape(-1)[:, None] * inv_freq
    cos, sin = jnp.cos(ang), jnp.sin(ang)
    t = ang.shape[0]
    pad = jnp.zeros((t, DA_QK_DIM - ROPE_DIM), F32)
    c = jnp.concatenate([cos, cos, pad + 1.0], axis=1)
    s1 = jnp.concatenate([-sin, jnp.zeros_like(sin), pad], axis=1)
    s2 = jnp.concatenate([jnp.zeros_like(sin), sin, pad], axis=1)
    rep = lambda a: jnp.concatenate([a] * (LANES // DA_QK_DIM), axis=1)
    return rep(c), rep(s1), rep(s2)


def kernel(x, positions, attn_norm_w, w_in, diff_lambda, diff_subln_w, rel_bias, conv_dw_w, conv_dw_b, conv_ln_w,
           conv_ln_b, w_out, ffn_norm_w, ffn_w1, ffn_w3, ffn_w2, moe_router_w, moe_router_b, moe_w1, moe_w3, moe_w2,
           final_norm_w):
    bsz, seq, d = x.shape
    depth = w_in.shape[0]
    t = bsz * seq
    da_width, cb_width = d // 2, d // 4
    cv = d - da_width - cb_width
    da_heads = da_width // DA_V_DIM
    qk_cols = da_heads * 2 * DA_QK_DIM
    dims = dict(qa=qk_cols, ka=qk_cols, da_heads=da_heads, qb_lo=2 * qk_cols + da_width,
                qb_hi=2 * qk_cols + da_width + cb_width, cb_heads=cb_width // CB_HEAD_DIM,
                cv_lo=2 * qk_cols + da_width + 3 * cb_width, cv=cv)
    rope_c, rope_s1, rope_s2 = rope_tables(positions)
    bias = band_bias(rel_bias.reshape((-1,) + rel_bias.shape[2:]), BAND_TQ)
    w_in, w_out, ffn_w2, moe_w2 = (w.astype(BF16) for w in (w_in, w_out, ffn_w2, moe_w2))
    ffn_w1, ffn_w3, moe_w1, moe_w3 = (tile_up_weights(w) for w in (ffn_w1, ffn_w3, moe_w1, moe_w3))
    h = x.reshape(t, d)
    u = rmsnorm(h, attn_norm_w[0], BF16)
    for l in range(depth):
        last = l == depth - 1
        moe = l % 2 == 1
        proj = in_projection(u, w_in, l, rope_c, rope_s1, rope_s2, dims)
        lam_init = 0.8 - 0.6 * math.exp(-0.3 * l)
        oa = diff_attention(proj, diff_lambda[l], diff_subln_w[l], lam_init, dims, bsz, seq)
        ob = band_attention(proj, bias, l, dims, bsz, seq)
        oc = conv_module(proj, conv_dw_w[l], conv_dw_b[l], conv_ln_w[l], conv_ln_b[l], dims, bsz, seq)
        h, u = out_projection(h, oa, ob, oc, w_out, l, ffn_norm_w[l], F32 if moe else BF16)
        next_w = final_norm_w if last else attn_norm_w[(l + 1) % depth]
        next_dtype = F32 if last else BF16
        i = l // 2
        if moe:
            h, u = moe_layer(u, h, moe_router_w[i], moe_router_b[i], moe_w1, moe_w3, moe_w2, i, next_w, next_dtype)
        else:
            h, u = ffn_dense(u, h, ffn_w1, ffn_w3, ffn_w2, i, next_w, next_dtype)
    return u.reshape(bsz, seq, d)
```

```python
import functools
import math

import jax
import jax.numpy as jnp
import numpy as np
from jax import lax
from jax.experimental import pallas as pl
from jax.experimental.pallas import tpu as pltpu

F32 = jnp.float32
BF16 = jnp.bfloat16

CHUNK = 64
DA_QK_DIM = 64
DA_V_DIM = 2 * DA_QK_DIM
CB_HEAD_DIM = 128
CB_LEFT_CHUNKS = 8
REL_CLIP = 128
CONV_WIDTH = 31
ROPE_THETA = 500000.0
ROPE_DIM = DA_QK_DIM // 4
TOP_K = 2
NORM_EPS = 1e-6
NEG = -1e30

LANES = 128
SUBLANES = 8
V7X_VMEM_LIMIT_BYTES = 60000 * 1024

CONV_HALO = 32


def _params(*sem, flags=None):
    return pltpu.CompilerParams(dimension_semantics=sem, vmem_limit_bytes=V7X_VMEM_LIMIT_BYTES, flags=flags)


def _tile(n, target):
    if n <= target:
        return n
    t = target
    while n % t:
        t -= 8
    return t


def _sigmoid(x):
    return 1.0 / (1.0 + jnp.exp(-x))


def _rms(x, w):
    return x * lax.rsqrt(jnp.mean(x * x, axis=-1, keepdims=True) + NORM_EPS) * w


def _rmsnorm_kernel(x_ref, w_ref, o_ref):
    o_ref[...] = _rms(x_ref[...], w_ref[...]).astype(o_ref.dtype)


def rmsnorm(x, w, out_dtype):
    t, d = x.shape
    tm = _tile(t, 1024)
    return pl.pallas_call(
        _rmsnorm_kernel,
        out_shape=jax.ShapeDtypeStruct((t, d), out_dtype),
        grid=(t // tm,),
        in_specs=[pl.BlockSpec((tm, d), lambda i: (i, 0)), pl.BlockSpec((1, d), lambda i: (0, 0))],
        out_specs=pl.BlockSpec((tm, d), lambda i: (i, 0)),
        compiler_params=_params("parallel"),
        name="rmsnorm",
    )(x, w.reshape(1, d))


def _rope_tile(acc, c, s1, s2):
    half = ROPE_DIM // 2
    parts = []
    for g in range(acc.shape[1] // LANES):
        x = acc[:, g * LANES:(g + 1) * LANES]
        parts.append(x * c + pltpu.roll(x, LANES - half, 1) * s1 + pltpu.roll(x, half, 1) * s2)
    return jnp.concatenate(parts, axis=1)


def _inproj_kernel(u_ref, w_ref, c_ref, s1_ref, s2_ref, o_ref, *, tn, qa_tiles, ka_tiles, qb_lo, qb_hi,
                   qa_scale, qb_scale):
    u = u_ref[...]
    for j in range(o_ref.shape[1] // tn):
        cols = slice(j * tn, (j + 1) * tn)
        acc = jnp.dot(u, w_ref[:, cols], preferred_element_type=F32)
        if j < qa_tiles:
            acc = _rope_tile(acc, c_ref[...], s1_ref[...], s2_ref[...]) * qa_scale
        elif j < qa_tiles + ka_tiles:
            acc = _rope_tile(acc, c_ref[...], s1_ref[...], s2_ref[...])
        elif qb_lo <= j < qb_hi:
            acc = acc * qb_scale
        o_ref[:, cols] = acc.astype(o_ref.dtype)


def in_projection(u, w, layer, rope_c, rope_s1, rope_s2, dims):
    t, d = u.shape
    n = w.shape[2]
    tm = _tile(t, 512)
    tn = 512
    assert n % tn == 0 and dims["qa"] % tn == 0 and dims["ka"] % tn == 0 and dims["qb_lo"] % tn == 0 \
        and dims["qb_hi"] % tn == 0
    kern = functools.partial(
        _inproj_kernel, tn=tn, qa_tiles=dims["qa"] // tn, ka_tiles=dims["ka"] // tn, qb_lo=dims["qb_lo"] // tn,
        qb_hi=dims["qb_hi"] // tn, qa_scale=DA_QK_DIM ** -0.5 * math.log2(math.e), qb_scale=CB_HEAD_DIM ** -0.5)
    tab = pl.BlockSpec((tm, LANES), lambda i: (i, 0))
    return pl.pallas_call(
        kern,
        out_shape=jax.ShapeDtypeStruct((t, n), BF16),
        grid=(t // tm,),
        in_specs=[pl.BlockSpec((tm, d), lambda i: (i, 0)),
                  pl.BlockSpec((None, d, n), lambda i: (layer, 0, 0), pipeline_mode=pl.Buffered(1)),
                  tab, tab, tab],
        out_specs=pl.BlockSpec((tm, n), lambda i: (i, 0)),
        compiler_params=_params("parallel"),
        name="in_projection",
    )(u, w, rope_c, rope_s1, rope_s2)


DA_SUM_ROWS = 16


def _diffattn_kernel(q_ref, k_ref, v_ref, lam_ref, sw_ref, o_ref, qc_ref, vt_ref, s_ref, m_ref, acc_ref, *, tq,
                     lam_init):
    i = pl.program_id(2)

    @pl.when(i == 0)
    def _():
        def transpose_block(c, carry):
            start = pl.multiple_of(c * tq, tq)
            vt_ref[c, 0:DA_V_DIM, :] = v_ref[pl.ds(start, tq), :].astype(F32).T.astype(BF16)
            vt_ref[c, DA_V_DIM:, :] = jnp.ones((DA_SUM_ROWS, tq), BF16)
            return carry

        lax.fori_loop(0, vt_ref.shape[0], transpose_block, 0)

    q = q_ref[...]
    lane = lax.broadcasted_iota(jnp.int32, q.shape, 1)
    zero = jnp.zeros_like(q)
    qc_ref[0:tq, :] = jnp.where(lane < DA_QK_DIM, q, zero)
    qc_ref[tq:2 * tq, :] = jnp.where(lane >= DA_QK_DIM, q, zero)
    m_ref[...] = jnp.full(m_ref.shape, NEG, F32)
    acc_ref[...] = jnp.zeros(acc_ref.shape, F32)

    def scores(j):
        k = k_ref[pl.ds(pl.multiple_of(j * tq, tq), tq), :]
        return lax.dot_general(k, qc_ref[...], (((1,), (1,)), ((), ())), preferred_element_type=F32)

    def accumulate(j, slot, masked):
        s = s_ref[slot]
        if masked:
            kc = lax.broadcasted_iota(jnp.int32, s.shape, 0) // CHUNK
            qi = lax.broadcasted_iota(jnp.int32, s.shape, 1)
            qcx = jnp.where(qi >= tq, qi - tq, qi) // CHUNK
            s = jnp.where(kc <= qcx, s, NEG)
        m_prev = m_ref[...]
        m_new = jnp.maximum(m_prev, jnp.max(s, axis=0, keepdims=True))
        alpha = jnp.exp2(m_prev - m_new)
        p = jnp.exp2(s - m_new).astype(BF16)
        acc_ref[...] = alpha * acc_ref[...] + jnp.dot(vt_ref[j], p, preferred_element_type=F32)
        m_ref[...] = m_new

    s_ref[0] = scores(0)

    def body(jj, carry):
        j0 = 2 * jj
        s_ref[1] = scores(j0 + 1)
        accumulate(j0, 0, False)
        s_ref[0] = scores(j0 + 2)
        accumulate(j0 + 1, 1, False)
        return carry

    lax.fori_loop(0, i // 2, body, 0)

    @pl.when(i % 2 == 1)
    def _():
        s_ref[1] = scores(i)
        accumulate(i - 1, 0, False)
        accumulate(i, 1, True)

    @pl.when(i % 2 == 0)
    def _():
        accumulate(i, 0, True)

    lv = lam_ref[...]
    lam = (jnp.exp(jnp.sum(lv[0:1] * lv[1:2], keepdims=True))
           - jnp.exp(jnp.sum(lv[2:3] * lv[3:4], keepdims=True)) + lam_init)
    o_t = acc_ref[0:DA_V_DIM, :] / acc_ref[DA_V_DIM:DA_V_DIM + 1, :]
    o = (o_t[:, 0:tq] - lam * o_t[:, tq:2 * tq]).T
    o_ref[...] = (_rms(o, sw_ref[...]) * (1.0 - lam_init)).astype(o_ref.dtype)


def diff_attention(proj, lam_vec, subln_w, lam_init, dims, bsz, seq):
    heads = dims["da_heads"]
    tq = _tile(seq, 512)
    p3 = proj.reshape(bsz, seq, proj.shape[-1])
    kcol = dims["qa"] // LANES
    vcol = (dims["qa"] + dims["ka"]) // LANES
    kern = functools.partial(_diffattn_kernel, tq=tq, lam_init=lam_init)
    out = pl.pallas_call(
        kern,
        out_shape=jax.ShapeDtypeStruct((bsz, seq, heads * DA_V_DIM), BF16),
        grid=(bsz, heads, seq // tq),
        in_specs=[
            pl.BlockSpec((None, tq, LANES), lambda b, h, i: (b, i, h)),
            pl.BlockSpec((None, seq, LANES), lambda b, h, i: (b, 0, kcol + h)),
            pl.BlockSpec((None, seq, LANES), lambda b, h, i: (b, 0, vcol + h)),
            pl.BlockSpec(lam_vec.shape, lambda b, h, i: (0, 0)),
            pl.BlockSpec((1, DA_V_DIM), lambda b, h, i: (0, 0)),
        ],
        out_specs=pl.BlockSpec((None, tq, DA_V_DIM), lambda b, h, i: (b, i, h)),
        scratch_shapes=[pltpu.VMEM((2 * tq, LANES), BF16),
                        pltpu.VMEM((seq // tq, DA_V_DIM + DA_SUM_ROWS, tq), BF16),
                        pltpu.VMEM((2, tq, 2 * tq), F32),
                        pltpu.VMEM((1, 2 * tq), F32), pltpu.VMEM((DA_V_DIM + DA_SUM_ROWS, 2 * tq), F32)],
        compiler_params=_params("parallel", "parallel", "arbitrary"),
        name="diff_attention",
    )(p3, p3, p3, lam_vec, subln_w.reshape(1, DA_V_DIM))
    return out.reshape(bsz * seq, heads * DA_V_DIM)


def _bandattn_kernel(q_ref, kp_ref, kc_ref, vp_ref, vc_ref, bias_ref, o_ref, *, tq):
    i = pl.program_id(2)
    k = jnp.concatenate([kp_ref[...], kc_ref[...]], axis=0)
    v = jnp.concatenate([vp_ref[...], vc_ref[...]], axis=0)
    s = lax.dot_general(q_ref[...], k, (((1,), (1,)), ((), ())), preferred_element_type=F32) + bias_ref[...]
    col = lax.broadcasted_iota(jnp.int32, s.shape, 1)
    s = jnp.where((i == 0) & (col < tq), NEG, s)
    p = jnp.exp(s - jnp.max(s, axis=-1, keepdims=True))
    p = p / jnp.sum(p, axis=-1, keepdims=True)
    o_ref[...] = jnp.dot(p.astype(BF16), v, preferred_element_type=F32).astype(o_ref.dtype)


def band_bias(rel_bias, tq):
    assert tq > REL_CLIP
    n_h = rel_bias.shape[0]
    p = 3 * tq
    rb = rel_bias.astype(F32)
    far = jnp.broadcast_to(rb[:, 2 * REL_CLIP:], (n_h, p))
    behind = jnp.broadcast_to(rb[:, :1], (n_h, p))
    r = jnp.concatenate([far[:, :tq - REL_CLIP + 1], rb[:, 1:2 * REL_CLIP][:, ::-1], behind[:, :tq - REL_CLIP],
                         far[:, :tq]], axis=1)
    toep = jnp.tile(r, (1, tq))[:, :tq * (p - 1)].reshape(n_h, tq, p - 1)[:, :, :2 * tq]
    qc = np.arange(tq)[:, None] // CHUNK
    kc = np.arange(2 * tq)[None, :] // CHUNK
    visible = (kc >= qc) & (kc <= qc + CB_LEFT_CHUNKS)
    return jnp.where(visible[None], toep, NEG)


BAND_TQ = CB_LEFT_CHUNKS * CHUNK


def band_attention(proj, bias, layer, dims, bsz, seq):
    heads = dims["cb_heads"]
    tq = BAND_TQ
    assert seq % tq == 0
    p3 = proj.reshape(bsz, seq, proj.shape[-1])
    qcol = dims["qb_lo"] // LANES
    kcol = qcol + heads
    vcol = kcol + heads
    prev = lambda col: pl.BlockSpec((None, tq, LANES), lambda b, h, i: (b, jnp.maximum(i - 1, 0), col + h))
    cur = lambda col: pl.BlockSpec((None, tq, LANES), lambda b, h, i: (b, i, col + h))
    out = pl.pallas_call(
        functools.partial(_bandattn_kernel, tq=tq),
        out_shape=jax.ShapeDtypeStruct((bsz, seq, heads * CB_HEAD_DIM), BF16),
        grid=(bsz, heads, seq // tq),
        in_specs=[cur(qcol), prev(kcol), cur(kcol), prev(vcol), cur(vcol),
                  pl.BlockSpec((None, tq, 2 * tq), lambda b, h, i: (layer * heads + h, 0, 0))],
        out_specs=pl.BlockSpec((None, tq, CB_HEAD_DIM), lambda b, h, i: (b, i, h)),
        compiler_params=_params("parallel", "parallel", "arbitrary"),
        name="band_attention",
    )(p3, p3, p3, p3, p3, bias)
    return out.reshape(bsz * seq, heads * CB_HEAD_DIM)


def _conv_kernel(a_ref, g_ref, ap_ref, gp_ref, w_ref, b_ref, lnw_ref, lnb_ref, o_ref, u_ref, *, tc, sub):
    i = pl.program_id(1)
    up = ap_ref[...].astype(F32) * _sigmoid(gp_ref[...].astype(F32))
    u_ref[0, 0:CONV_HALO, :] = jnp.where(i > 0, up, 0.0)
    u_ref[0, CONV_HALO:CONV_HALO + tc, :] = a_ref[...].astype(F32) * _sigmoid(g_ref[...].astype(F32))
    n_shift = CONV_HALO + tc - SUBLANES
    for s in range(1, SUBLANES):
        u_ref[s, 0:n_shift, :] = u_ref[0, s:s + n_shift, :]
    first = CONV_HALO - (CONV_WIDTH - 1)

    def body(r, carry):
        r0 = pl.multiple_of(r * sub, sub)
        acc = jnp.zeros((sub, u_ref.shape[2]), F32) + b_ref[...]
        for j in range(CONV_WIDTH):
            off = first + j
            acc = acc + w_ref[j:j + 1, :] * u_ref[off % SUBLANES, pl.ds(r0 + off - off % SUBLANES, sub), :]
        xc = acc - jnp.mean(acc, axis=-1, keepdims=True)
        y = xc * lax.rsqrt(jnp.mean(xc * xc, axis=-1, keepdims=True) + NORM_EPS) * lnw_ref[...] + lnb_ref[...]
        o_ref[pl.ds(r0, sub), :] = (y * _sigmoid(y)).astype(o_ref.dtype)
        return carry

    lax.fori_loop(0, tc // sub, body, 0)


def conv_module(proj, dw_w, dw_b, ln_w, ln_b, dims, bsz, seq):
    c = dims["cv"]
    tc = _tile(seq, 512)
    sub = 32
    p3 = proj.reshape(bsz, seq, proj.shape[-1])
    acol = dims["cv_lo"] // c
    gcol = acol + 1
    per_blk = tc // CONV_HALO
    cur = lambda col: pl.BlockSpec((None, tc, c), lambda b, i: (b, i, col))
    prev = lambda col: pl.BlockSpec((None, CONV_HALO, c), lambda b, i: (b, jnp.maximum(i * per_blk - 1, 0), col))
    row = pl.BlockSpec((1, c), lambda b, i: (0, 0))
    out = pl.pallas_call(
        functools.partial(_conv_kernel, tc=tc, sub=sub),
        out_shape=jax.ShapeDtypeStruct((bsz, seq, c), BF16),
        grid=(bsz, seq // tc),
        in_specs=[cur(acol), cur(gcol), prev(acol), prev(gcol),
                  pl.BlockSpec((CONV_WIDTH, c), lambda b, i: (0, 0)), row, row, row],
        out_specs=pl.BlockSpec((None, tc, c), lambda b, i: (b, i, 0)),
        scratch_shapes=[pltpu.VMEM((SUBLANES, CONV_HALO + tc, c), F32)],
        compiler_params=_params("parallel", "arbitrary"),
        name="conv_module",
    )(p3, p3, p3, p3, dw_w, dw_b.reshape(1, c), ln_w.reshape(1, c), ln_b.reshape(1, c))
    return out.reshape(bsz * seq, c)


def _outproj_kernel(h_ref, oa_ref, ob_ref, oc_ref, w_ref, nw_ref, hn_ref, u_ref, *, ka, kb):
    acc = h_ref[...]
    acc = acc + jnp.dot(oa_ref[...], w_ref[0:ka, :].astype(BF16), preferred_element_type=F32)
    acc = acc + jnp.dot(ob_ref[...], w_ref[ka:ka + kb, :].astype(BF16), preferred_element_type=F32)
    acc = acc + jnp.dot(oc_ref[...], w_ref[ka + kb:, :].astype(BF16), preferred_element_type=F32)
    hn_ref[...] = acc
    u_ref[...] = _rms(acc, nw_ref[...]).astype(u_ref.dtype)


def out_projection(h, oa, ob, oc, w, layer, norm_w, u_dtype):
    t, d = h.shape
    tm = _tile(t, 512)
    ka, kb, kc = oa.shape[1], ob.shape[1], oc.shape[1]
    rows = lambda k: pl.BlockSpec((tm, k), lambda i: (i, 0))
    return pl.pallas_call(
        functools.partial(_outproj_kernel, ka=ka, kb=kb),
        out_shape=(jax.ShapeDtypeStruct((t, d), F32), jax.ShapeDtypeStruct((t, d), u_dtype)),
        grid=(t // tm,),
        in_specs=[rows(d), rows(ka), rows(kb), rows(kc),
                  pl.BlockSpec((None,) + w.shape[1:], lambda i: (layer, 0, 0)),
                  pl.BlockSpec((1, d), lambda i: (0, 0))],
        out_specs=(rows(d), rows(d)),
        compiler_params=_params("parallel"),
        name="out_projection",
    )(h, oa, ob, oc, w, norm_w.reshape(1, d))


FFN_ROW_TILE = 1024


def _swiglu_steps(f, nf, x_ref, w1_ref, w3_ref, w2_ref, g_ref, acc_ref):
    def gated():
        x = x_ref[...]
        h1 = jnp.dot(x, w1_ref[...], preferred_element_type=F32)
        h3 = jnp.dot(x, w3_ref[...], preferred_element_type=F32)
        return (h1 * _sigmoid(h1) * h3).astype(g_ref.dtype)

    @pl.when(f == 0)
    def _():
        g_ref[...] = gated()

    @pl.when((f > 0) & (f < nf))
    def _():
        part = jnp.dot(g_ref[...], w2_ref[...], preferred_element_type=F32)
        g_new = gated()
        acc_ref[...] += part
        g_ref[...] = g_new

    @pl.when(f == nf)
    def _():
        acc_ref[...] += jnp.dot(g_ref[...], w2_ref[...], preferred_element_type=F32)


def _ffn_dense_kernel(x_ref, h_any, w1_ref, w3_ref, w2_ref, nw_ref, hn_ref, u_ref, g_ref, sem):
    i = pl.program_id(0)
    f = pl.program_id(1)
    nf = pl.num_programs(1) - 1
    tm = hn_ref.shape[0]
    residual = pltpu.make_async_copy(h_any.at[pl.ds(pl.multiple_of(i * tm, tm), tm)], hn_ref, sem)

    @pl.when(f == 0)
    def _():
        residual.start()

    @pl.when(f == 1)
    def _():
        residual.wait()

    _swiglu_steps(f, nf, x_ref, w1_ref, w3_ref, w2_ref, g_ref, hn_ref)

    @pl.when(f == nf)
    def _():
        u_ref[...] = _rms(hn_ref[...], nw_ref[...]).astype(u_ref.dtype)


def ffn_dense(u, h, w1, w3, w2, idx, norm_w, u_dtype):
    t, d = u.shape
    dff = w1.shape[2]
    tm = _tile(t, FFN_ROW_TILE)
    tf = _tile(dff, 512)
    nf = dff // tf
    rows = pl.BlockSpec((tm, d), lambda i, f: (i, 0))
    w_up = pl.BlockSpec((None, d, tf), lambda i, f: (idx, 0, jnp.minimum(f, nf - 1)))
    w_down = pl.BlockSpec((None, tf, d), lambda i, f: (idx, jnp.maximum(f - 1, 0), 0))
    return pl.pallas_call(
        _ffn_dense_kernel,
        out_shape=(jax.ShapeDtypeStruct((t, d), F32), jax.ShapeDtypeStruct((t, d), u_dtype)),
        grid=(t // tm, nf + 1),
        in_specs=[rows, pl.BlockSpec(memory_space=pl.ANY), w_up, w_up, w_down,
                  pl.BlockSpec((1, d), lambda i, f: (0, 0))],
        out_specs=(rows, rows),
        scratch_shapes=[pltpu.VMEM((tm, tf), BF16), pltpu.SemaphoreType.DMA(())],
        compiler_params=_params("arbitrary", "arbitrary"),
        name="ffn_dense",
    )(u, h, w1, w3, w2, norm_w.reshape(1, d))


def _ffn_grouped_kernel(te_ref, nv_ref, x_ref, w1_ref, w3_ref, w2_ref, y_ref, xb_ref, g_ref):
    i = pl.program_id(0)
    f = pl.program_id(1)
    nf = pl.num_programs(1) - 1

    @pl.when(f == 0)
    def _():
        y_ref[...] = jnp.zeros(y_ref.shape, y_ref.dtype)

    @pl.when(i < nv_ref[0])
    def _():
        @pl.when(f == 0)
        def _():
            xb_ref[...] = x_ref[...].astype(xb_ref.dtype)

        _swiglu_steps(f, nf, xb_ref, w1_ref, w3_ref, w2_ref, g_ref, y_ref)


def ffn_grouped(xs, tile_expert, n_valid, w1, w3, w2, idx, tm):
    p, d = xs.shape
    dff = w1.shape[3]
    tf = _tile(dff, 512)
    nf = dff // tf
    up_idx = lambda i, f, nv: jnp.where(i < nv[0], jnp.minimum(f, nf - 1), nf - 1)
    down_idx = lambda i, f, nv: jnp.where(i < nv[0], jnp.maximum(f - 1, 0), nf - 1)
    w_up = pl.BlockSpec((None, None, d, tf), lambda i, f, te, nv: (idx, te[i], 0, up_idx(i, f, nv)))
    grid_spec = pltpu.PrefetchScalarGridSpec(
        num_scalar_prefetch=2,
        grid=(p // tm, nf + 1),
        in_specs=[
            pl.BlockSpec((tm, d), lambda i, f, te, nv: (jnp.minimum(i, nv[0] - 1), 0), pipeline_mode=pl.Buffered(1)),
            w_up, w_up,
            pl.BlockSpec((None, None, tf, d), lambda i, f, te, nv: (idx, te[i], down_idx(i, f, nv), 0)),
        ],
        out_specs=pl.BlockSpec((tm, d), lambda i, f, te, nv: (i, 0)),
        scratch_shapes=[pltpu.VMEM((tm, d), BF16), pltpu.VMEM((tm, tf), BF16)],
    )
    return pl.pallas_call(
        _ffn_grouped_kernel,
        out_shape=jax.ShapeDtypeStruct((p, d), F32),
        grid_spec=grid_spec,
        compiler_params=_params("arbitrary", "arbitrary"),
        name="ffn_grouped",
    )(tile_expert, n_valid, xs, w1, w3, w2)


ROUTE_E1, ROUTE_E2, ROUTE_R1, ROUTE_R2, ROUTE_G1, ROUTE_G2 = range(6)


def _router_kernel(u_ref, rw_ref, rb_ref, info_ref, cnt_ref, carry_ref, *, n_exp):
    @pl.when(pl.program_id(0) == 0)
    def _():
        carry_ref[...] = jnp.zeros(carry_ref.shape, F32)

    logits = jnp.dot(u_ref[...].astype(BF16), rw_ref[...].astype(BF16), preferred_element_type=F32) + rb_ref[...]
    tm = logits.shape[0]
    lane = lax.broadcasted_iota(jnp.int32, logits.shape, 1).astype(F32)
    logits = jnp.where(lane < n_exp, logits, -jnp.inf)
    m1 = jnp.max(logits, axis=-1, keepdims=True)
    i1 = jnp.min(jnp.where(logits == m1, lane, float(LANES)), axis=-1, keepdims=True)
    rest = jnp.where(lane == i1, -jnp.inf, logits)
    m2 = jnp.max(rest, axis=-1, keepdims=True)
    i2 = jnp.min(jnp.where(rest == m2, lane, float(LANES)), axis=-1, keepdims=True)
    e = jnp.exp(m2 - m1)
    g1 = 1.0 / (1.0 + e)
    g2 = e / (1.0 + e)
    oh1 = (lane == i1).astype(F32)
    oh2 = (lane == i2).astype(F32)
    oh = oh1 + oh2
    r = lax.broadcasted_iota(jnp.int32, (tm, tm), 0)
    c = lax.broadcasted_iota(jnp.int32, (tm, tm), 1)
    before = jnp.dot((c < r).astype(BF16), oh.astype(BF16), preferred_element_type=F32) + carry_ref[...]
    r1 = jnp.sum(before * oh1, axis=-1, keepdims=True)
    r2 = jnp.sum(before * oh2, axis=-1, keepdims=True)
    carry_ref[...] += jnp.sum(oh, axis=0, keepdims=True)
    info = jnp.zeros(logits.shape, F32)
    for slot, val in ((ROUTE_E1, i1), (ROUTE_E2, i2), (ROUTE_R1, r1), (ROUTE_R2, r2), (ROUTE_G1, g1),
                      (ROUTE_G2, g2)):
        info = jnp.where(lane == slot, val, info)
    info_ref[...] = info
    cnt_ref[...] = carry_ref[...]


def router(u, rw, rb):
    t, d = u.shape
    n_exp = rw.shape[1]
    tm = _tile(t, 512)
    rw_p = jnp.zeros((d, LANES), rw.dtype).at[:, :n_exp].set(rw)
    rb_p = jnp.zeros((1, LANES), F32).at[0, :n_exp].set(rb.astype(F32))
    return pl.pallas_call(
        functools.partial(_router_kernel, n_exp=n_exp),
        out_shape=(jax.ShapeDtypeStruct((t, LANES), F32), jax.ShapeDtypeStruct((1, LANES), F32)),
        grid=(t // tm,),
        in_specs=[pl.BlockSpec((tm, d), lambda i: (i, 0)), pl.BlockSpec((d, LANES), lambda i: (0, 0)),
                  pl.BlockSpec((1, LANES), lambda i: (0, 0))],
        out_specs=(pl.BlockSpec((tm, LANES), lambda i: (i, 0)), pl.BlockSpec((1, LANES), lambda i: (0, 0))),
        scratch_shapes=[pltpu.VMEM((1, LANES), F32)],
        compiler_params=_params("arbitrary"),
        name="router",
    )(u, rw_p, rb_p)


def _row_copy(src, si, dst, di, sem):
    return pltpu.make_async_copy(src.at[pl.ds(si, 1)], dst.at[pl.ds(di, 1)], sem)


def _dispatch_kernel(dest_ref, u_ref, xs_in_ref, xs_ref, sem, *, td):
    del xs_in_ref

    def start(t, carry):
        for k in range(TOP_K):
            _row_copy(u_ref, t, xs_ref, dest_ref[k, t], sem).start()
        return carry

    def wait(t, carry):
        for k in range(TOP_K):
            _row_copy(u_ref, t, xs_ref, dest_ref[k, t], sem).wait()
        return carry

    lax.fori_loop(0, td, start, 0)
    lax.fori_loop(0, td, wait, 0)


def dispatch(u, dest, xs_zero):
    t, d = u.shape
    td = dest.shape[-1]
    return pl.pallas_call(
        functools.partial(_dispatch_kernel, td=td),
        out_shape=jax.ShapeDtypeStruct(xs_zero.shape, xs_zero.dtype),
        grid=(t // td,),
        in_specs=[pl.BlockSpec((None, TOP_K, td), lambda i: (i, 0, 0), memory_space=pltpu.SMEM),
                  pl.BlockSpec((td, d), lambda i: (i, 0)),
                  pl.BlockSpec(memory_space=pl.ANY)],
        out_specs=pl.BlockSpec(memory_space=pl.ANY),
        scratch_shapes=[pltpu.SemaphoreType.DMA(())],
        input_output_aliases={2: 0},
        compiler_params=_params("arbitrary"),
        name="moe_dispatch",
    )(dest, u, xs_zero)


def _combine_kernel(dest_ref, h_ref, info_ref, ys_ref, nw_ref, hn_ref, u_ref, buf_ref, sem, *, td):
    def start(t, carry):
        for k in range(TOP_K):
            _row_copy(ys_ref, dest_ref[k, t], buf_ref.at[k], t, sem).start()
        return carry

    def wait(t, carry):
        for k in range(TOP_K):
            _row_copy(ys_ref, dest_ref[k, t], buf_ref.at[k], t, sem).wait()
        return carry

    lax.fori_loop(0, td, start, 0)
    lax.fori_loop(0, td, wait, 0)
    info = info_ref[...]
    g1 = info[:, ROUTE_G1:ROUTE_G1 + 1]
    g2 = info[:, ROUTE_G2:ROUTE_G2 + 1]
    hn = h_ref[...] + g1 * buf_ref[0] + g2 * buf_ref[1]
    hn_ref[...] = hn
    u_ref[...] = _rms(hn, nw_ref[...]).astype(u_ref.dtype)


def combine(h, info, dest, ys, norm_w, u_dtype):
    t, d = h.shape
    td = dest.shape[-1]
    rows = pl.BlockSpec((td, d), lambda i: (i, 0))
    return pl.pallas_call(
        functools.partial(_combine_kernel, td=td),
        out_shape=(jax.ShapeDtypeStruct((t, d), F32), jax.ShapeDtypeStruct((t, d), u_dtype)),
        grid=(t // td,),
        in_specs=[pl.BlockSpec((None, TOP_K, td), lambda i: (i, 0, 0), memory_space=pltpu.SMEM),
                  rows, pl.BlockSpec((td, LANES), lambda i: (i, 0)), pl.BlockSpec(memory_space=pl.ANY),
                  pl.BlockSpec((1, d), lambda i: (0, 0))],
        out_specs=(rows, rows),
        scratch_shapes=[pltpu.VMEM((TOP_K, td, d), F32), pltpu.SemaphoreType.DMA(())],
        compiler_params=_params("arbitrary"),
        name="moe_combine",
    )(dest, h, info, ys, norm_w.reshape(1, d))


MOE_ROW_TILE = FFN_ROW_TILE
MOE_TOKEN_TILE = 256


def moe_layer(u, h, rw, rb, w1, w3, w2, idx, norm_w, u_dtype):
    t, d = u.shape
    n_exp = rw.shape[1]
    tm = MOE_ROW_TILE
    td = _tile(t, MOE_TOKEN_TILE)
    info, cnt = router(u, rw, rb)
    counts = cnt[0, :n_exp].astype(jnp.int32)
    padded = ((counts + tm - 1) // tm) * tm
    ends = jnp.cumsum(padded)
    starts = ends - padded
    e = info[:, ROUTE_E1:ROUTE_E2 + 1].astype(jnp.int32)
    rank = info[:, ROUTE_R1:ROUTE_R2 + 1].astype(jnp.int32)
    dest = (starts[e] + rank).T.reshape(TOP_K, t // td, td).transpose(1, 0, 2)
    n_rows = TOP_K * t + n_exp * tm
    n_tiles = n_rows // tm
    n_valid = (ends[-1] // tm).astype(jnp.int32)
    tile_start = jnp.minimum(jnp.arange(n_tiles, dtype=jnp.int32), n_valid - 1) * tm
    tile_expert = jnp.minimum(jnp.sum(tile_start[:, None] >= ends[None, :], axis=1), n_exp - 1).astype(jnp.int32)
    xs = dispatch(u, dest, jnp.zeros((n_rows, d), u.dtype))
    ys = ffn_grouped(xs, tile_expert, n_valid.reshape(1), w1, w3, w2, idx, tm)
    return combine(h, info, dest, ys, norm_w, u_dtype)


def rope_tables(positions):
    half = ROPE_DIM // 2
    inv_freq = ROPE_THETA ** (-jnp.arange(0, ROPE_DIM, 2, dtype=F32) / ROPE_DIM)
    ang = positions.astype(F32).reshape(-1)[:, None] * inv_freq
    cos, sin = jnp.cos(ang), jnp.sin(ang)
    t = ang.shape[0]
    pad = jnp.zeros((t, DA_QK_DIM - ROPE_DIM), F32)
    c = jnp.concatenate([cos, cos, pad + 1.0], axis=1)
    s1 = jnp.concatenate([-sin, jnp.zeros_like(sin), pad], axis=1)
    s2 = jnp.concatenate([jnp.zeros_like(sin), sin, pad], axis=1)
    rep = lambda a: jnp.concatenate([a] * (LANES // DA_QK_DIM), axis=1)
    return rep(c), rep(s1), rep(s2)


def kernel(x, positions, attn_norm_w, w_in, diff_lambda, diff_subln_w, rel_bias, conv_dw_w, conv_dw_b, conv_ln_w,
           conv_ln_b, w_out, ffn_norm_w, ffn_w1, ffn_w3, ffn_w2, moe_router_w, moe_router_b, moe_w1, moe_w3, moe_w2,
           final_norm_w):
    bsz, seq, d = x.shape
    depth = w_in.shape[0]
    t = bsz * seq
    da_width, cb_width = d // 2, d // 4
    cv = d - da_width - cb_width
    da_heads = da_width // DA_V_DIM
    qk_cols = da_heads * 2 * DA_QK_DIM
    dims = dict(qa=qk_cols, ka=qk_cols, da_heads=da_heads, qb_lo=2 * qk_cols + da_width,
                qb_hi=2 * qk_cols + da_width + cb_width, cb_heads=cb_width // CB_HEAD_DIM,
                cv_lo=2 * qk_cols + da_width + 3 * cb_width, cv=cv)
    rope_c, rope_s1, rope_s2 = rope_tables(positions)
    bias = band_bias(rel_bias.reshape((-1,) + rel_bias.shape[2:]), BAND_TQ)
    w_in, w_out, ffn_w1, ffn_w3, ffn_w2, moe_w1, moe_w3, moe_w2 = (
        w.astype(BF16) for w in (w_in, w_out, ffn_w1, ffn_w3, ffn_w2, moe_w1, moe_w3, moe_w2))
    h = x.reshape(t, d)
    u = rmsnorm(h, attn_norm_w[0], BF16)
    for l in range(depth):
        last = l == depth - 1
        moe = l % 2 == 1
        proj = in_projection(u, w_in, l, rope_c, rope_s1, rope_s2, dims)
        lam_init = 0.8 - 0.6 * math.exp(-0.3 * l)
        oa = diff_attention(proj, diff_lambda[l], diff_subln_w[l], lam_init, dims, bsz, seq)
        ob = band_attention(proj, bias, l, dims, bsz, seq)
        oc = conv_module(proj, conv_dw_w[l], conv_dw_b[l], conv_ln_w[l], conv_ln_b[l], dims, bsz, seq)
        h, u = out_projection(h, oa, ob, oc, w_out, l, ffn_norm_w[l], F32 if moe else BF16)
        next_w = final_norm_w if last else attn_norm_w[(l + 1) % depth]
        next_dtype = F32 if last else BF16
        i = l // 2
        if moe:
            h, u = moe_layer(u, h, moe_router_w[i], moe_router_b[i], moe_w1, moe_w3, moe_w2, i, next_w, next_dtype)
        else:
            h, u = ffn_dense(u, h, ffn_w1, ffn_w3, ffn_w2, i, next_w, next_dtype)
    return u.reshape(bsz, seq, d)
```

```python
import functools
import math

import jax
import jax.numpy as jnp
import numpy as np
from jax import lax
from jax.experimental import pallas as pl
from jax.experimental.pallas import tpu as pltpu

F32 = jnp.float32
BF16 = jnp.bfloat16

CHUNK = 64
DA_QK_DIM = 64
DA_V_DIM = 2 * DA_QK_DIM
CB_HEAD_DIM = 128
CB_LEFT_CHUNKS = 8
REL_CLIP = 128
CONV_WIDTH = 31
ROPE_THETA = 500000.0
ROPE_DIM = DA_QK_DIM // 4
TOP_K = 2
NORM_EPS = 1e-6
NEG = -1e30

LANES = 128
SUBLANES = 8
V7X_VMEM_LIMIT_BYTES = 60000 * 1024

CONV_HALO = 32


def _params(*sem, flags=None):
    return pltpu.CompilerParams(dimension_semantics=sem, vmem_limit_bytes=V7X_VMEM_LIMIT_BYTES, flags=flags)


def _tile(n, target):
    if n <= target:
        return n
    t = target
    while n % t:
        t -= 8
    return t


def _sigmoid(x):
    return 1.0 / (1.0 + jnp.exp(-x))


def _rms(x, w):
    return x * lax.rsqrt(jnp.mean(x * x, axis=-1, keepdims=True) + NORM_EPS) * w


def _rmsnorm_kernel(x_ref, w_ref, o_ref):
    o_ref[...] = _rms(x_ref[...], w_ref[...]).astype(o_ref.dtype)


def rmsnorm(x, w, out_dtype):
    t, d = x.shape
    tm = _tile(t, 1024)
    return pl.pallas_call(
        _rmsnorm_kernel,
        out_shape=jax.ShapeDtypeStruct((t, d), out_dtype),
        grid=(t // tm,),
        in_specs=[pl.BlockSpec((tm, d), lambda i: (i, 0)), pl.BlockSpec((1, d), lambda i: (0, 0))],
        out_specs=pl.BlockSpec((tm, d), lambda i: (i, 0)),
        compiler_params=_params("parallel"),
        name="rmsnorm",
    )(x, w.reshape(1, d))


def _rope_tile(acc, c, s1, s2):
    half = ROPE_DIM // 2
    parts = []
    for g in range(acc.shape[1] // LANES):
        x = acc[:, g * LANES:(g + 1) * LANES]
        parts.append(x * c + pltpu.roll(x, LANES - half, 1) * s1 + pltpu.roll(x, half, 1) * s2)
    return jnp.concatenate(parts, axis=1)


def _inproj_kernel(u_ref, w_ref, c_ref, s1_ref, s2_ref, o_ref, *, tn, qa_tiles, ka_tiles, qb_lo, qb_hi,
                   qa_scale, qb_scale):
    u = u_ref[...]
    for j in range(o_ref.shape[1] // tn):
        cols = slice(j * tn, (j + 1) * tn)
        acc = jnp.dot(u, w_ref[:, cols], preferred_element_type=F32)
        if j < qa_tiles:
            acc = _rope_tile(acc, c_ref[...], s1_ref[...], s2_ref[...]) * qa_scale
        elif j < qa_tiles + ka_tiles:
            acc = _rope_tile(acc, c_ref[...], s1_ref[...], s2_ref[...])
        elif qb_lo <= j < qb_hi:
            acc = acc * qb_scale
        o_ref[:, cols] = acc.astype(o_ref.dtype)


def in_projection(u, w, layer, rope_c, rope_s1, rope_s2, dims):
    t, d = u.shape
    n = w.shape[2]
    tm = _tile(t, 512)
    tn = 512
    assert n % tn == 0 and dims["qa"] % tn == 0 and dims["ka"] % tn == 0 and dims["qb_lo"] % tn == 0 \
        and dims["qb_hi"] % tn == 0
    kern = functools.partial(
        _inproj_kernel, tn=tn, qa_tiles=dims["qa"] // tn, ka_tiles=dims["ka"] // tn, qb_lo=dims["qb_lo"] // tn,
        qb_hi=dims["qb_hi"] // tn, qa_scale=DA_QK_DIM ** -0.5 * math.log2(math.e), qb_scale=CB_HEAD_DIM ** -0.5)
    tab = pl.BlockSpec((tm, LANES), lambda i: (i, 0))
    return pl.pallas_call(
        kern,
        out_shape=jax.ShapeDtypeStruct((t, n), BF16),
        grid=(t // tm,),
        in_specs=[pl.BlockSpec((tm, d), lambda i: (i, 0)),
                  pl.BlockSpec((None, d, n), lambda i: (layer, 0, 0), pipeline_mode=pl.Buffered(1)),
                  tab, tab, tab],
        out_specs=pl.BlockSpec((tm, n), lambda i: (i, 0)),
        compiler_params=_params("parallel"),
        name="in_projection",
    )(u, w, rope_c, rope_s1, rope_s2)


DA_SUM_ROWS = 16


def _diffattn_kernel(q_ref, k_ref, v_ref, lam_ref, sw_ref, o_ref, qc_ref, vt_ref, s_ref, m_ref, acc_ref, *, tq,
                     lam_init):
    i = pl.program_id(2)

    @pl.when(i == 0)
    def _():
        def transpose_block(c, carry):
            start = pl.multiple_of(c * tq, tq)
            vt_ref[c, 0:DA_V_DIM, :] = v_ref[pl.ds(start, tq), :].astype(F32).T.astype(BF16)
            vt_ref[c, DA_V_DIM:, :] = jnp.ones((DA_SUM_ROWS, tq), BF16)
            return carry

        lax.fori_loop(0, vt_ref.shape[0], transpose_block, 0)

    q = q_ref[...]
    lane = lax.broadcasted_iota(jnp.int32, q.shape, 1)
    zero = jnp.zeros_like(q)
    qc_ref[0:tq, :] = jnp.where(lane < DA_QK_DIM, q, zero)
    qc_ref[tq:2 * tq, :] = jnp.where(lane >= DA_QK_DIM, q, zero)
    m_ref[...] = jnp.full(m_ref.shape, NEG, F32)
    acc_ref[...] = jnp.zeros(acc_ref.shape, F32)

    def scores(j):
        k = k_ref[pl.ds(pl.multiple_of(j * tq, tq), tq), :]
        return lax.dot_general(k, qc_ref[...], (((1,), (1,)), ((), ())), preferred_element_type=F32)

    def accumulate(j, slot, masked):
        s = s_ref[slot]
        if masked:
            kc = lax.broadcasted_iota(jnp.int32, s.shape, 0) // CHUNK
            qi = lax.broadcasted_iota(jnp.int32, s.shape, 1)
            qcx = jnp.where(qi >= tq, qi - tq, qi) // CHUNK
            s = jnp.where(kc <= qcx, s, NEG)
        m_prev = m_ref[...]
        m_new = jnp.maximum(m_prev, jnp.max(s, axis=0, keepdims=True))
        alpha = jnp.exp2(m_prev - m_new)
        p = jnp.exp2(s - m_new).astype(BF16)
        acc_ref[...] = alpha * acc_ref[...] + jnp.dot(vt_ref[j], p, preferred_element_type=F32)
        m_ref[...] = m_new

    s_ref[0] = scores(0)

    def body(jj, carry):
        j0 = 2 * jj
        s_ref[1] = scores(j0 + 1)
        accumulate(j0, 0, False)
        s_ref[0] = scores(j0 + 2)
        accumulate(j0 + 1, 1, False)
        return carry

    lax.fori_loop(0, i // 2, body, 0)

    @pl.when(i % 2 == 1)
    def _():
        s_ref[1] = scores(i)
        accumulate(i - 1, 0, False)
        accumulate(i, 1, True)

    @pl.when(i % 2 == 0)
    def _():
        accumulate(i, 0, True)

    lv = lam_ref[...]
    lam = (jnp.exp(jnp.sum(lv[0:1] * lv[1:2], keepdims=True))
           - jnp.exp(jnp.sum(lv[2:3] * lv[3:4], keepdims=True)) + lam_init)
    o_t = acc_ref[0:DA_V_DIM, :] / acc_ref[DA_V_DIM:DA_V_DIM + 1, :]
    o = (o_t[:, 0:tq] - lam * o_t[:, tq:2 * tq]).T
    o_ref[...] = (_rms(o, sw_ref[...]) * (1.0 - lam_init)).astype(o_ref.dtype)


def diff_attention(proj, lam_vec, subln_w, lam_init, dims, bsz, seq):
    heads = dims["da_heads"]
    tq = _tile(seq, 512)
    p3 = proj.reshape(bsz, seq, proj.shape[-1])
    kcol = dims["qa"] // LANES
    vcol = (dims["qa"] + dims["ka"]) // LANES
    kern = functools.partial(_diffattn_kernel, tq=tq, lam_init=lam_init)
    out = pl.pallas_call(
        kern,
        out_shape=jax.ShapeDtypeStruct((bsz, seq, heads * DA_V_DIM), BF16),
        grid=(bsz, heads, seq // tq),
        in_specs=[
            pl.BlockSpec((None, tq, LANES), lambda b, h, i: (b, i, h)),
            pl.BlockSpec((None, seq, LANES), lambda b, h, i: (b, 0, kcol + h)),
            pl.BlockSpec((None, seq, LANES), lambda b, h, i: (b, 0, vcol + h)),
            pl.BlockSpec(lam_vec.shape, lambda b, h, i: (0, 0)),
            pl.BlockSpec((1, DA_V_DIM), lambda b, h, i: (0, 0)),
        ],
        out_specs=pl.BlockSpec((None, tq, DA_V_DIM), lambda b, h, i: (b, i, h)),
        scratch_shapes=[pltpu.VMEM((2 * tq, LANES), BF16),
                        pltpu.VMEM((seq // tq, DA_V_DIM + DA_SUM_ROWS, tq), BF16),
                        pltpu.VMEM((2, tq, 2 * tq), F32),
                        pltpu.VMEM((1, 2 * tq), F32), pltpu.VMEM((DA_V_DIM + DA_SUM_ROWS, 2 * tq), F32)],
        compiler_params=_params("parallel", "parallel", "arbitrary"),
        name="diff_attention",
    )(p3, p3, p3, lam_vec, subln_w.reshape(1, DA_V_DIM))
    return out.reshape(bsz * seq, heads * DA_V_DIM)


def _bandattn_kernel(q_ref, kp_ref, kc_ref, vp_ref, vc_ref, bias_ref, o_ref, *, tq):
    i = pl.program_id(2)
    k = jnp.concatenate([kp_ref[...], kc_ref[...]], axis=0)
    v = jnp.concatenate([vp_ref[...], vc_ref[...]], axis=0)
    sub = 2 * CHUNK
    span = sub + CB_LEFT_CHUNKS * CHUNK
    for lo in range(0, tq, sub):
        s = lax.dot_general(q_ref[lo:lo + sub, :], k[lo:lo + span], (((1,), (1,)), ((), ())),
                            preferred_element_type=F32) + bias_ref[lo:lo + sub, lo:lo + span]
        col = lo + lax.broadcasted_iota(jnp.int32, s.shape, 1)
        s = jnp.where((i == 0) & (col < tq), NEG, s)
        p = jnp.exp(s - jnp.max(s, axis=-1, keepdims=True))
        p = p / jnp.sum(p, axis=-1, keepdims=True)
        o_ref[lo:lo + sub, :] = jnp.dot(p.astype(BF16), v[lo:lo + span],
                                        preferred_element_type=F32).astype(o_ref.dtype)


def band_bias(rel_bias, tq):
    assert tq > REL_CLIP
    n_h = rel_bias.shape[0]
    p = 3 * tq
    rb = rel_bias.astype(F32)
    far = jnp.broadcast_to(rb[:, 2 * REL_CLIP:], (n_h, p))
    behind = jnp.broadcast_to(rb[:, :1], (n_h, p))
    r = jnp.concatenate([far[:, :tq - REL_CLIP + 1], rb[:, 1:2 * REL_CLIP][:, ::-1], behind[:, :tq - REL_CLIP],
                         far[:, :tq]], axis=1)
    toep = jnp.tile(r, (1, tq))[:, :tq * (p - 1)].reshape(n_h, tq, p - 1)[:, :, :2 * tq]
    qc = np.arange(tq)[:, None] // CHUNK
    kc = np.arange(2 * tq)[None, :] // CHUNK
    visible = (kc >= qc) & (kc <= qc + CB_LEFT_CHUNKS)
    return jnp.where(visible[None], toep, NEG)


BAND_TQ = CB_LEFT_CHUNKS * CHUNK


def band_attention(proj, bias, layer, dims, bsz, seq):
    heads = dims["cb_heads"]
    tq = BAND_TQ
    assert seq % tq == 0
    p3 = proj.reshape(bsz, seq, proj.shape[-1])
    qcol = dims["qb_lo"] // LANES
    kcol = qcol + heads
    vcol = kcol + heads
    prev = lambda col: pl.BlockSpec((None, tq, LANES), lambda b, h, i: (b, jnp.maximum(i - 1, 0), col + h))
    cur = lambda col: pl.BlockSpec((None, tq, LANES), lambda b, h, i: (b, i, col + h))
    out = pl.pallas_call(
        functools.partial(_bandattn_kernel, tq=tq),
        out_shape=jax.ShapeDtypeStruct((bsz, seq, heads * CB_HEAD_DIM), BF16),
        grid=(bsz, heads, seq // tq),
        in_specs=[cur(qcol), prev(kcol), cur(kcol), prev(vcol), cur(vcol),
                  pl.BlockSpec((None, tq, 2 * tq), lambda b, h, i: (layer * heads + h, 0, 0))],
        out_specs=pl.BlockSpec((None, tq, CB_HEAD_DIM), lambda b, h, i: (b, i, h)),
        compiler_params=_params("parallel", "parallel", "arbitrary"),
        name="band_attention",
    )(p3, p3, p3, p3, p3, bias)
    return out.reshape(bsz * seq, heads * CB_HEAD_DIM)


def _conv_kernel(a_ref, g_ref, ap_ref, gp_ref, w_ref, b_ref, lnw_ref, lnb_ref, o_ref, u_ref, *, tc, sub):
    i = pl.program_id(1)
    up = ap_ref[...].astype(F32) * _sigmoid(gp_ref[...].astype(F32))
    u_ref[0, 0:CONV_HALO, :] = jnp.where(i > 0, up, 0.0)
    u_ref[0, CONV_HALO:CONV_HALO + tc, :] = a_ref[...].astype(F32) * _sigmoid(g_ref[...].astype(F32))
    n_shift = CONV_HALO + tc - SUBLANES
    for s in range(1, SUBLANES):
        u_ref[s, 0:n_shift, :] = u_ref[0, s:s + n_shift, :]
    first = CONV_HALO - (CONV_WIDTH - 1)

    def body(r, carry):
        r0 = pl.multiple_of(r * sub, sub)
        acc = jnp.zeros((sub, u_ref.shape[2]), F32) + b_ref[...]
        for j in range(CONV_WIDTH):
            off = first + j
            acc = acc + w_ref[j:j + 1, :] * u_ref[off % SUBLANES, pl.ds(r0 + off - off % SUBLANES, sub), :]
        xc = acc - jnp.mean(acc, axis=-1, keepdims=True)
        y = xc * lax.rsqrt(jnp.mean(xc * xc, axis=-1, keepdims=True) + NORM_EPS) * lnw_ref[...] + lnb_ref[...]
        o_ref[pl.ds(r0, sub), :] = (y * _sigmoid(y)).astype(o_ref.dtype)
        return carry

    lax.fori_loop(0, tc // sub, body, 0)


def conv_module(proj, dw_w, dw_b, ln_w, ln_b, dims, bsz, seq):
    c = dims["cv"]
    tc = _tile(seq, 512)
    sub = 32
    p3 = proj.reshape(bsz, seq, proj.shape[-1])
    acol = dims["cv_lo"] // c
    gcol = acol + 1
    per_blk = tc // CONV_HALO
    cur = lambda col: pl.BlockSpec((None, tc, c), lambda b, i: (b, i, col))
    prev = lambda col: pl.BlockSpec((None, CONV_HALO, c), lambda b, i: (b, jnp.maximum(i * per_blk - 1, 0), col))
    row = pl.BlockSpec((1, c), lambda b, i: (0, 0))
    out = pl.pallas_call(
        functools.partial(_conv_kernel, tc=tc, sub=sub),
        out_shape=jax.ShapeDtypeStruct((bsz, seq, c), BF16),
        grid=(bsz, seq // tc),
        in_specs=[cur(acol), cur(gcol), prev(acol), prev(gcol),
                  pl.BlockSpec((CONV_WIDTH, c), lambda b, i: (0, 0)), row, row, row],
        out_specs=pl.BlockSpec((None, tc, c), lambda b, i: (b, i, 0)),
        scratch_shapes=[pltpu.VMEM((SUBLANES, CONV_HALO + tc, c), F32)],
        compiler_params=_params("parallel", "arbitrary"),
        name="conv_module",
    )(p3, p3, p3, p3, dw_w, dw_b.reshape(1, c), ln_w.reshape(1, c), ln_b.reshape(1, c))
    return out.reshape(bsz * seq, c)


def _outproj_kernel(h_ref, oa_ref, ob_ref, oc_ref, w_ref, nw_ref, hn_ref, u_ref, *, ka, kb):
    acc = h_ref[...]
    acc = acc + jnp.dot(oa_ref[...], w_ref[0:ka, :].astype(BF16), preferred_element_type=F32)
    acc = acc + jnp.dot(ob_ref[...], w_ref[ka:ka + kb, :].astype(BF16), preferred_element_type=F32)
    acc = acc + jnp.dot(oc_ref[...], w_ref[ka + kb:, :].astype(BF16), preferred_element_type=F32)
    hn_ref[...] = acc
    u_ref[...] = _rms(acc, nw_ref[...]).astype(u_ref.dtype)


def out_projection(h, oa, ob, oc, w, layer, norm_w, u_dtype):
    t, d = h.shape
    tm = _tile(t, 512)
    ka, kb, kc = oa.shape[1], ob.shape[1], oc.shape[1]
    rows = lambda k: pl.BlockSpec((tm, k), lambda i: (i, 0))
    return pl.pallas_call(
        functools.partial(_outproj_kernel, ka=ka, kb=kb),
        out_shape=(jax.ShapeDtypeStruct((t, d), F32), jax.ShapeDtypeStruct((t, d), u_dtype)),
        grid=(t // tm,),
        in_specs=[rows(d), rows(ka), rows(kb), rows(kc),
                  pl.BlockSpec((None,) + w.shape[1:], lambda i: (layer, 0, 0)),
                  pl.BlockSpec((1, d), lambda i: (0, 0))],
        out_specs=(rows(d), rows(d)),
        compiler_params=_params("parallel"),
        name="out_projection",
    )(h, oa, ob, oc, w, norm_w.reshape(1, d))


FFN_ROW_TILE = 1024


def _swiglu_steps(f, nf, x_ref, w1_ref, w3_ref, w2_ref, g_ref, acc_ref):
    def gated():
        x = x_ref[...]
        h1 = jnp.dot(x, w1_ref[...], preferred_element_type=F32)
        h3 = jnp.dot(x, w3_ref[...], preferred_element_type=F32)
        return (h1 * _sigmoid(h1) * h3).astype(g_ref.dtype)

    @pl.when(f == 0)
    def _():
        g_ref[...] = gated()

    @pl.when((f > 0) & (f < nf))
    def _():
        part = jnp.dot(g_ref[...], w2_ref[...], preferred_element_type=F32)
        g_new = gated()
        acc_ref[...] += part
        g_ref[...] = g_new

    @pl.when(f == nf)
    def _():
        acc_ref[...] += jnp.dot(g_ref[...], w2_ref[...], preferred_element_type=F32)


def _ffn_dense_kernel(x_ref, h_any, w1_ref, w3_ref, w2_ref, nw_ref, hn_ref, u_ref, g_ref, sem):
    i = pl.program_id(0)
    f = pl.program_id(1)
    nf = pl.num_programs(1) - 1
    tm = hn_ref.shape[0]
    residual = pltpu.make_async_copy(h_any.at[pl.ds(pl.multiple_of(i * tm, tm), tm)], hn_ref, sem)

    @pl.when(f == 0)
    def _():
        residual.start()

    @pl.when(f == 1)
    def _():
        residual.wait()

    _swiglu_steps(f, nf, x_ref, w1_ref, w3_ref, w2_ref, g_ref, hn_ref)

    @pl.when(f == nf)
    def _():
        u_ref[...] = _rms(hn_ref[...], nw_ref[...]).astype(u_ref.dtype)


def ffn_dense(u, h, w1, w3, w2, idx, norm_w, u_dtype):
    t, d = u.shape
    dff = w1.shape[2]
    tm = _tile(t, FFN_ROW_TILE)
    tf = _tile(dff, 512)
    nf = dff // tf
    rows = pl.BlockSpec((tm, d), lambda i, f: (i, 0))
    w_up = pl.BlockSpec((None, d, tf), lambda i, f: (idx, 0, jnp.minimum(f, nf - 1)))
    w_down = pl.BlockSpec((None, tf, d), lambda i, f: (idx, jnp.maximum(f - 1, 0), 0))
    return pl.pallas_call(
        _ffn_dense_kernel,
        out_shape=(jax.ShapeDtypeStruct((t, d), F32), jax.ShapeDtypeStruct((t, d), u_dtype)),
        grid=(t // tm, nf + 1),
        in_specs=[rows, pl.BlockSpec(memory_space=pl.ANY), w_up, w_up, w_down,
                  pl.BlockSpec((1, d), lambda i, f: (0, 0))],
        out_specs=(rows, rows),
        scratch_shapes=[pltpu.VMEM((tm, tf), BF16), pltpu.SemaphoreType.DMA(())],
        compiler_params=_params("arbitrary", "arbitrary"),
        name="ffn_dense",
    )(u, h, w1, w3, w2, norm_w.reshape(1, d))


def _ffn_grouped_kernel(te_ref, nv_ref, x_ref, w1_ref, w3_ref, w2_ref, y_ref, xb_ref, g_ref):
    i = pl.program_id(0)
    f = pl.program_id(1)
    nf = pl.num_programs(1) - 1

    @pl.when(f == 0)
    def _():
        y_ref[...] = jnp.zeros(y_ref.shape, y_ref.dtype)

    @pl.when(i < nv_ref[0])
    def _():
        @pl.when(f == 0)
        def _():
            xb_ref[...] = x_ref[...].astype(xb_ref.dtype)

        _swiglu_steps(f, nf, xb_ref, w1_ref, w3_ref, w2_ref, g_ref, y_ref)


def ffn_grouped(xs, tile_expert, n_valid, w1, w3, w2, idx, tm):
    p, d = xs.shape
    dff = w1.shape[3]
    tf = _tile(dff, 512)
    nf = dff // tf
    up_idx = lambda i, f, nv: jnp.where(i < nv[0], jnp.minimum(f, nf - 1), nf - 1)
    down_idx = lambda i, f, nv: jnp.where(i < nv[0], jnp.maximum(f - 1, 0), nf - 1)
    w_up = pl.BlockSpec((None, None, d, tf), lambda i, f, te, nv: (idx, te[i], 0, up_idx(i, f, nv)))
    grid_spec = pltpu.PrefetchScalarGridSpec(
        num_scalar_prefetch=2,
        grid=(p // tm, nf + 1),
        in_specs=[
            pl.BlockSpec((tm, d), lambda i, f, te, nv: (jnp.minimum(i, nv[0] - 1), 0), pipeline_mode=pl.Buffered(1)),
            w_up, w_up,
            pl.BlockSpec((None, None, tf, d), lambda i, f, te, nv: (idx, te[i], down_idx(i, f, nv), 0)),
        ],
        out_specs=pl.BlockSpec((tm, d), lambda i, f, te, nv: (i, 0)),
        scratch_shapes=[pltpu.VMEM((tm, d), BF16), pltpu.VMEM((tm, tf), BF16)],
    )
    return pl.pallas_call(
        _ffn_grouped_kernel,
        out_shape=jax.ShapeDtypeStruct((p, d), F32),
        grid_spec=grid_spec,
        compiler_params=_params("arbitrary", "arbitrary"),
        name="ffn_grouped",
    )(tile_expert, n_valid, xs, w1, w3, w2)


ROUTE_E1, ROUTE_E2, ROUTE_R1, ROUTE_R2, ROUTE_G1, ROUTE_G2 = range(6)


def _router_kernel(u_ref, rw_ref, rb_ref, info_ref, cnt_ref, carry_ref, *, n_exp):
    @pl.when(pl.program_id(0) == 0)
    def _():
        carry_ref[...] = jnp.zeros(carry_ref.shape, F32)

    logits = jnp.dot(u_ref[...].astype(BF16), rw_ref[...].astype(BF16), preferred_element_type=F32) + rb_ref[...]
    tm = logits.shape[0]
    lane = lax.broadcasted_iota(jnp.int32, logits.shape, 1).astype(F32)
    logits = jnp.where(lane < n_exp, logits, -jnp.inf)
    m1 = jnp.max(logits, axis=-1, keepdims=True)
    i1 = jnp.min(jnp.where(logits == m1, lane, float(LANES)), axis=-1, keepdims=True)
    rest = jnp.where(lane == i1, -jnp.inf, logits)
    m2 = jnp.max(rest, axis=-1, keepdims=True)
    i2 = jnp.min(jnp.where(rest == m2, lane, float(LANES)), axis=-1, keepdims=True)
    e = jnp.exp(m2 - m1)
    g1 = 1.0 / (1.0 + e)
    g2 = e / (1.0 + e)
    oh1 = (lane == i1).astype(F32)
    oh2 = (lane == i2).astype(F32)
    oh = oh1 + oh2
    r = lax.broadcasted_iota(jnp.int32, (tm, tm), 0)
    c = lax.broadcasted_iota(jnp.int32, (tm, tm), 1)
    before = jnp.dot((c < r).astype(BF16), oh.astype(BF16), preferred_element_type=F32) + carry_ref[...]
    r1 = jnp.sum(before * oh1, axis=-1, keepdims=True)
    r2 = jnp.sum(before * oh2, axis=-1, keepdims=True)
    carry_ref[...] += jnp.sum(oh, axis=0, keepdims=True)
    info = jnp.zeros(logits.shape, F32)
    for slot, val in ((ROUTE_E1, i1), (ROUTE_E2, i2), (ROUTE_R1, r1), (ROUTE_R2, r2), (ROUTE_G1, g1),
                      (ROUTE_G2, g2)):
        info = jnp.where(lane == slot, val, info)
    info_ref[...] = info
    cnt_ref[...] = carry_ref[...]


def router(u, rw, rb):
    t, d = u.shape
    n_exp = rw.shape[1]
    tm = _tile(t, 512)
    rw_p = jnp.zeros((d, LANES), rw.dtype).at[:, :n_exp].set(rw)
    rb_p = jnp.zeros((1, LANES), F32).at[0, :n_exp].set(rb.astype(F32))
    return pl.pallas_call(
        functools.partial(_router_kernel, n_exp=n_exp),
        out_shape=(jax.ShapeDtypeStruct((t, LANES), F32), jax.ShapeDtypeStruct((1, LANES), F32)),
        grid=(t // tm,),
        in_specs=[pl.BlockSpec((tm, d), lambda i: (i, 0)), pl.BlockSpec((d, LANES), lambda i: (0, 0)),
                  pl.BlockSpec((1, LANES), lambda i: (0, 0))],
        out_specs=(pl.BlockSpec((tm, LANES), lambda i: (i, 0)), pl.BlockSpec((1, LANES), lambda i: (0, 0))),
        scratch_shapes=[pltpu.VMEM((1, LANES), F32)],
        compiler_params=_params("arbitrary"),
        name="router",
    )(u, rw_p, rb_p)


def _row_copy(src, si, dst, di, sem):
    return pltpu.make_async_copy(src.at[pl.ds(si, 1)], dst.at[pl.ds(di, 1)], sem)


def _dispatch_kernel(dest_ref, u_ref, xs_in_ref, xs_ref, sem, *, td):
    del xs_in_ref

    def start(t, carry):
        for k in range(TOP_K):
            _row_copy(u_ref, t, xs_ref, dest_ref[k, t], sem).start(priority=k)
        return carry

    def wait(t, carry):
        for k in range(TOP_K):
            _row_copy(u_ref, t, xs_ref, dest_ref[k, t], sem).wait()
        return carry

    lax.fori_loop(0, td, start, 0)
    lax.fori_loop(0, td, wait, 0)


def dispatch(u, dest, xs_zero):
    t, d = u.shape
    td = dest.shape[-1]
    return pl.pallas_call(
        functools.partial(_dispatch_kernel, td=td),
        out_shape=jax.ShapeDtypeStruct(xs_zero.shape, xs_zero.dtype),
        grid=(t // td,),
        in_specs=[pl.BlockSpec((None, TOP_K, td), lambda i: (i, 0, 0), memory_space=pltpu.SMEM),
                  pl.BlockSpec((td, d), lambda i: (i, 0)),
                  pl.BlockSpec(memory_space=pl.ANY)],
        out_specs=pl.BlockSpec(memory_space=pl.ANY),
        scratch_shapes=[pltpu.SemaphoreType.DMA(())],
        input_output_aliases={2: 0},
        compiler_params=_params("arbitrary"),
        name="moe_dispatch",
    )(dest, u, xs_zero)


def _combine_kernel(dest_ref, h_ref, info_ref, ys_ref, nw_ref, hn_ref, u_ref, buf_ref, sem, *, td):
    def start(t, carry):
        for k in range(TOP_K):
            _row_copy(ys_ref, dest_ref[k, t], buf_ref.at[k], t, sem).start(priority=k)
        return carry

    def wait(t, carry):
        for k in range(TOP_K):
            _row_copy(ys_ref, dest_ref[k, t], buf_ref.at[k], t, sem).wait()
        return carry

    lax.fori_loop(0, td, start, 0)
    lax.fori_loop(0, td, wait, 0)
    info = info_ref[...]
    g1 = info[:, ROUTE_G1:ROUTE_G1 + 1]
    g2 = info[:, ROUTE_G2:ROUTE_G2 + 1]
    hn = h_ref[...] + g1 * buf_ref[0] + g2 * buf_ref[1]
    hn_ref[...] = hn
    u_ref[...] = _rms(hn, nw_ref[...]).astype(u_ref.dtype)


def combine(h, info, dest, ys, norm_w, u_dtype):
    t, d = h.shape
    td = dest.shape[-1]
    rows = pl.BlockSpec((td, d), lambda i: (i, 0))
    return pl.pallas_call(
        functools.partial(_combine_kernel, td=td),
        out_shape=(jax.ShapeDtypeStruct((t, d), F32), jax.ShapeDtypeStruct((t, d), u_dtype)),
        grid=(t // td,),
        in_specs=[pl.BlockSpec((None, TOP_K, td), lambda i: (i, 0, 0), memory_space=pltpu.SMEM),
                  rows, pl.BlockSpec((td, LANES), lambda i: (i, 0)), pl.BlockSpec(memory_space=pl.ANY),
                  pl.BlockSpec((1, d), lambda i: (0, 0))],
        out_specs=(rows, rows),
        scratch_shapes=[pltpu.VMEM((TOP_K, td, d), F32), pltpu.SemaphoreType.DMA(())],
        compiler_params=_params("arbitrary"),
        name="moe_combine",
    )(dest, h, info, ys, norm_w.reshape(1, d))


MOE_ROW_TILE = FFN_ROW_TILE
MOE_TOKEN_TILE = 256


def moe_layer(u, h, rw, rb, w1, w3, w2, idx, norm_w, u_dtype):
    t, d = u.shape
    n_exp = rw.shape[1]
    tm = MOE_ROW_TILE
    td = _tile(t, MOE_TOKEN_TILE)
    info, cnt = router(u, rw, rb)
    counts = cnt[0, :n_exp].astype(jnp.int32)
    padded = ((counts + tm - 1) // tm) * tm
    ends = jnp.cumsum(padded)
    starts = ends - padded
    e = info[:, ROUTE_E1:ROUTE_E2 + 1].astype(jnp.int32)
    rank = info[:, ROUTE_R1:ROUTE_R2 + 1].astype(jnp.int32)
    dest = (starts[e] + rank).T.reshape(TOP_K, t // td, td).transpose(1, 0, 2)
    n_rows = TOP_K * t + n_exp * tm
    n_tiles = n_rows // tm
    n_valid = (ends[-1] // tm).astype(jnp.int32)
    tile_start = jnp.minimum(jnp.arange(n_tiles, dtype=jnp.int32), n_valid - 1) * tm
    tile_expert = jnp.minimum(jnp.sum(tile_start[:, None] >= ends[None, :], axis=1), n_exp - 1).astype(jnp.int32)
    xs = dispatch(u, dest, jnp.zeros((n_rows, d), u.dtype))
    ys = ffn_grouped(xs, tile_expert, n_valid.reshape(1), w1, w3, w2, idx, tm)
    return combine(h, info, dest, ys, norm_w, u_dtype)


def rope_tables(positions):
    half = ROPE_DIM // 2
    inv_freq = ROPE_THETA ** (-jnp.arange(0, ROPE_DIM, 2, dtype=F32) / ROPE_DIM)
    ang = positions.astype(F32).reshape(-1)[:, None] * inv_freq
    cos, sin = jnp.cos(ang), jnp.sin(ang)
    t = ang.shape[0]
    pad = jnp.zeros((t, DA_QK_DIM - ROPE_DIM), F32)
    c = jnp.concatenate([cos, cos, pad + 1.0], axis=1)
    s1 = jnp.concatenate([-sin, jnp.zeros_like(sin), pad], axis=1)
    s2 = jnp.concatenate([jnp.zeros_like(sin), sin, pad], axis=1)
    rep = lambda a: jnp.concatenate([a] * (LANES // DA_QK_DIM), axis=1)
    return rep(c), rep(s1), rep(s2)


def kernel(x, positions, attn_norm_w, w_in, diff_lambda, diff_subln_w, rel_bias, conv_dw_w, conv_dw_b, conv_ln_w,
           conv_ln_b, w_out, ffn_norm_w, ffn_w1, ffn_w3, ffn_w2, moe_router_w, moe_router_b, moe_w1, moe_w3, moe_w2,
           final_norm_w):
    bsz, seq, d = x.shape
    depth = w_in.shape[0]
    t = bsz * seq
    da_width, cb_width = d // 2, d // 4
    cv = d - da_width - cb_width
    da_heads = da_width // DA_V_DIM
    qk_cols = da_heads * 2 * DA_QK_DIM
    dims = dict(qa=qk_cols, ka=qk_cols, da_heads=da_heads, qb_lo=2 * qk_cols + da_width,
                qb_hi=2 * qk_cols + da_width + cb_width, cb_heads=cb_width // CB_HEAD_DIM,
                cv_lo=2 * qk_cols + da_width + 3 * cb_width, cv=cv)
    rope_c, rope_s1, rope_s2 = rope_tables(positions)
    bias = band_bias(rel_bias.reshape((-1,) + rel_bias.shape[2:]), BAND_TQ)
    w_in, w_out, ffn_w1, ffn_w3, ffn_w2, moe_w1, moe_w3, moe_w2 = (
        w.astype(BF16) for w in (w_in, w_out, ffn_w1, ffn_w3, ffn_w2, moe_w1, moe_w3, moe_w2))
    h = x.reshape(t, d)
    u = rmsnorm(h, attn_norm_w[0], BF16)
    for l in range(depth):
        last = l == depth - 1
        moe = l % 2 == 1
        proj = in_projection(u, w_in, l, rope_c, rope_s1, rope_s2, dims)
        lam_init = 0.8 - 0.6 * math.exp(-0.3 * l)
        oa = diff_attention(proj, diff_lambda[l], diff_subln_w[l], lam_init, dims, bsz, seq)
        ob = band_attention(proj, bias, l, dims, bsz, seq)
        oc = conv_module(proj, conv_dw_w[l], conv_dw_b[l], conv_ln_w[l], conv_ln_b[l], dims, bsz, seq)
        h, u = out_projection(h, oa, ob, oc, w_out, l, ffn_norm_w[l], F32 if moe else BF16)
        next_w = final_norm_w if last else attn_norm_w[(l + 1) % depth]
        next_dtype = F32 if last else BF16
        i = l // 2
        if moe:
            h, u = moe_layer(u, h, moe_router_w[i], moe_router_b[i], moe_w1, moe_w3, moe_w2, i, next_w, next_dtype)
        else:
            h, u = ffn_dense(u, h, ffn_w1, ffn_w3, ffn_w2, i, next_w, next_dtype)
    return u.reshape(bsz, seq, d)
```
